```python
import math
import jax, jax.numpy as jnp
from jax import lax
import numpy as np

D_MODEL = 1024
BATCH = 8
SEQ = 2048
DEPTH = 1
DEC_BATCH = 128
DEC_SEQ = 4
PAST_LEN = 2048
PAGE_SIZE = 128

R_HEADS = 8
R_HEAD_DIM = 64
R_WIDTH = R_HEADS * R_HEAD_DIM
DECAY_LORA = 64
AAA_LORA = 64
GATE_LORA = 128
R_COLS = 3 * R_WIDTH + DECAY_LORA + AAA_LORA + GATE_LORA
A_HEADS = 4
A_QK_DIM = 64
A_V_DIM = 2 * A_QK_DIM
A_WIDTH = A_HEADS * A_V_DIM
A_COLS = A_HEADS * (2 * 2 * A_QK_DIM + A_V_DIM)
MIX_WIDTH = R_WIDTH + A_WIDTH
PROJ_COLS = A_COLS + R_COLS
D_FF = 2816
NUM_BUCKETS = 32
MAX_DISTANCE = 128
Q_BLOCK = 128
NORM_EPS = 1e-6
SUBLN_EPS = 1e-5
GN_EPS = 64e-5
N_SUBNORMS = 6
NEG_INF = -1e30

kernel_name = 'hybrid_rwkv7_diffattn_macaron_step'


def rms_norm(x, g, eps):
    xf = x.astype(jnp.float32)
    y = xf * lax.rsqrt(jnp.mean(xf * xf, axis=-1, keepdims=True) + eps)
    return (y * g.astype(jnp.float32)).astype(x.dtype)


def swiglu(x, w_in, w_out):
    gate, up = jnp.split(x @ w_in, 2, axis=-1)
    return (jax.nn.silu(gate) * up) @ w_out


def rel_bias(q_pos, k_pos, table):
    n = jnp.maximum(q_pos[:, None] - k_pos[None, :], 0)
    max_exact = NUM_BUCKETS // 2
    large = max_exact + (jnp.log(jnp.maximum(n, 1).astype(jnp.float32) / max_exact)
                         / math.log(MAX_DISTANCE / max_exact)
                         * (NUM_BUCKETS - max_exact)).astype(jnp.int32)
    large = jnp.minimum(large, NUM_BUCKETS - 1)
    bucket = jnp.where(n < max_exact, n, large)
    return jnp.transpose(table[bucket], (2, 0, 1)).astype(jnp.float32)


def diff_attend(q, k, v, q_pos, k_pos, table, lam):
    bias = rel_bias(q_pos, k_pos, table)
    s = jnp.einsum('bqhmd,bkhmd->bhmqk', q, k).astype(jnp.float32) * (A_QK_DIM ** -0.5)
    s = s + bias[None, :, None]
    s = jnp.where(k_pos[None, :] <= q_pos[:, None], s, NEG_INF)
    p = jax.nn.softmax(s, axis=-1)
    p = p[:, :, 0] - lam * p[:, :, 1]
    return jnp.einsum('bhqk,bkhd->bqhd', p.astype(v.dtype), v)


def diff_attend_blocked(q, k, v, table, lam):
    B, T = q.shape[0], q.shape[1]
    nb = T // Q_BLOCK
    qb = jnp.moveaxis(q.reshape(B, nb, Q_BLOCK, A_HEADS, 2, A_QK_DIM), 1, 0)
    k_pos = jnp.arange(T)
    starts = jnp.arange(nb) * Q_BLOCK

    def one(args):
        q_blk, s0 = args
        return diff_attend(q_blk, k, v, s0 + jnp.arange(Q_BLOCK), k_pos, table, lam)

    out = lax.map(one, (qb, starts))
    return jnp.moveaxis(out, 0, 1).reshape(B, T, A_HEADS, A_V_DIM)


def gather_pages(pool, page_table):
    g = pool[page_table]
    return g.reshape((page_table.shape[0], page_table.shape[1] * pool.shape[1]) + pool.shape[2:])


def wkv_scan(r, w, k, v, kk, a, s0):
    def step(S, inp):
        r_t, w_t, k_t, v_t, kk_t, a_t = inp
        sa = jnp.einsum('bhvk,bhk->bhv', S, -kk_t)
        S = (S * w_t[:, :, None, :] + sa[..., None] * (kk_t * a_t)[:, :, None, :]
             + v_t[..., None] * k_t[:, :, None, :])
        return S, jnp.einsum('bhvk,bhk->bhv', S, r_t)

    xs = tuple(jnp.moveaxis(t, 1, 0) for t in (r, w, k, v, kk, a))
    S, ys = lax.scan(step, s0.astype(jnp.float32), xs)
    return jnp.moveaxis(ys, 0, 1), S


def mixer(h, l, shift0, wkv0, attend, P):
    f32 = jnp.float32
    B, T = h.shape[0], h.shape[1]
    p = h @ P['w_in'][l]
    QK = A_HEADS * 2 * A_QK_DIM
    q = p[..., :QK].reshape(B, T, A_HEADS, 2, A_QK_DIM)
    k = p[..., QK:2 * QK].reshape(B, T, A_HEADS, 2, A_QK_DIM)
    v = p[..., 2 * QK:A_COLS].reshape(B, T, A_HEADS, A_V_DIM)
    lam_init = 0.8 - 0.6 * math.exp(-0.3 * l)
    lam = (jnp.exp(jnp.sum(P['da_lq1'][l].astype(f32) * P['da_lk1'][l].astype(f32)))
           - jnp.exp(jnp.sum(P['da_lq2'][l].astype(f32) * P['da_lk2'][l].astype(f32)))
           + lam_init)
    o = attend(q, k, v, lam)
    o_attn = (rms_norm(o, P['da_subln'][l], SUBLN_EPS) * (1.0 - lam_init)).reshape(B, T, A_WIDTH)
    pr = p[..., A_COLS:]
    prev = jnp.concatenate([shift0[:, None, :].astype(pr.dtype), pr[:, :-1]], axis=1)
    xr = pr + (prev - pr) * P['rw_mu'][l]
    r, kr, vr, w_lo, a_lo, g_lo = jnp.split(
        xr, [R_WIDTH, 2 * R_WIDTH, 3 * R_WIDTH, 3 * R_WIDTH + DECAY_LORA,
             3 * R_WIDTH + DECAY_LORA + AAA_LORA], axis=-1)
    log_w = -jax.nn.softplus(-(P['rw_w0'][l] + jnp.tanh(w_lo) @ P['rw_w2'][l])) - 0.5
    decay = jnp.exp(-jnp.exp(log_w.astype(f32)))
    a = jax.nn.sigmoid(P['rw_a0'][l] + a_lo @ P['rw_a2'][l])
    g = jax.nn.sigmoid(g_lo) @ P['rw_g2'][l]

    def heads(t):
        return t.reshape(B, T, R_HEADS, R_HEAD_DIM).astype(f32)

    kk = heads(kr * P['rw_kk'][l])
    kk = kk / jnp.maximum(jnp.sqrt(jnp.sum(kk * kk, axis=-1, keepdims=True)), 1e-12)
    kr = kr * (1.0 + (a - 1.0) * P['rw_ka'][l])
    rh, kh, vh, ah = heads(r), heads(kr), heads(vr), heads(a)
    y, wkv_new = wkv_scan(rh, heads(decay), kh, vh, kk, ah, wkv0)
    mu = jnp.mean(y, axis=-1, keepdims=True)
    var = jnp.mean(jnp.square(y - mu), axis=-1, keepdims=True)
    yn = ((y - mu) * lax.rsqrt(var + GN_EPS)).reshape(B, T, R_WIDTH)
    yn = yn * P['rw_gn_w'][l].astype(f32) + P['rw_gn_b'][l].astype(f32)
    bonus = jnp.sum(rh * kh * P['rw_rk'][l].astype(f32), axis=-1, keepdims=True) * vh
    o_rwkv = ((yn + bonus.reshape(B, T, R_WIDTH)) * g.astype(f32)).astype(h.dtype)
    out = jnp.concatenate([o_rwkv, o_attn.astype(h.dtype)], axis=-1) @ P['w_out'][l]
    return out, k, v, wkv_new, pr[:, -1]


def layer(x, l, shift0, wkv0, attend, P):
    g = P['norm_gains'][l]
    h = x + 0.5 * rms_norm(swiglu(rms_norm(x, g[0], NORM_EPS), P['ff1_in'][l], P['ff1_out'][l]), g[1], NORM_EPS)
    m, k, v, wkv, shift = mixer(rms_norm(h, g[2], NORM_EPS), l, shift0, wkv0, attend, P)
    h = h + rms_norm(m, g[3], NORM_EPS)
    h = h + 0.5 * rms_norm(swiglu(rms_norm(h, g[4], NORM_EPS), P['ff2_in'][l], P['ff2_out'][l]), g[5], NORM_EPS)
    return h, k, v, wkv, shift


def setup_inputs(seed: int = 0) -> dict:
    key = jax.random.key(seed)
    ks = iter(jax.random.split(key, 40))
    n_pages = PAST_LEN // PAGE_SIZE
    used = DEC_BATCH * n_pages
    n_pool = used + max(1, used // 4)

    def nrm(shape, scale):
        return jax.random.normal(next(ks), shape, jnp.float32) * scale

    inp = {}
    inp['x_prompt'] = nrm((BATCH, SEQ, D_MODEL), 1.0)
    inp['x_sample'] = nrm((DEC_BATCH, DEC_SEQ, D_MODEL), 1.0)
    inp['cache_k'] = nrm((DEPTH, n_pool, PAGE_SIZE, A_HEADS, 2, A_QK_DIM), 1.0)
    inp['cache_v'] = nrm((DEPTH, n_pool, PAGE_SIZE, A_HEADS, A_V_DIM), 1.0)
    inp['page_table'] = jax.random.permutation(next(ks), n_pool)[:used].reshape(DEC_BATCH, n_pages).astype(jnp.int32)
    inp['state_wkv'] = nrm((DEPTH, DEC_BATCH, R_HEADS, R_HEAD_DIM, R_HEAD_DIM), 0.3)
    inp['state_shift'] = nrm((DEPTH, DEC_BATCH, R_COLS), 1.0)
    inp['norm_gains'] = 1.0 + nrm((DEPTH, N_SUBNORMS, D_MODEL), 0.05)
    inp['ff1_in'] = nrm((DEPTH, D_MODEL, 2 * D_FF), D_MODEL ** -0.5)
    inp['ff1_out'] = nrm((DEPTH, D_FF, D_MODEL), D_FF ** -0.5)
    inp['ff2_in'] = nrm((DEPTH, D_MODEL, 2 * D_FF), D_MODEL ** -0.5)
    inp['ff2_out'] = nrm((DEPTH, D_FF, D_MODEL), D_FF ** -0.5)
    inp['w_in'] = nrm((DEPTH, D_MODEL, PROJ_COLS), D_MODEL ** -0.5)
    inp['w_out'] = nrm((DEPTH, MIX_WIDTH, D_MODEL), MIX_WIDTH ** -0.5)
    inp['rel_bias_table'] = nrm((NUM_BUCKETS, A_HEADS), 0.3)
    inp['da_lq1'] = nrm((DEPTH, A_QK_DIM), 0.1)
    inp['da_lk1'] = nrm((DEPTH, A_QK_DIM), 0.1)
    inp['da_lq2'] = nrm((DEPTH, A_QK_DIM), 0.1)
    inp['da_lk2'] = nrm((DEPTH, A_QK_DIM), 0.1)
    inp['da_subln'] = 1.0 + nrm((DEPTH, A_V_DIM), 0.05)
    inp['rw_mu'] = jax.random.uniform(next(ks), (DEPTH, R_COLS), jnp.float32)
    inp['rw_w0'] = nrm((DEPTH, R_WIDTH), 0.5) - 0.5
    inp['rw_w2'] = nrm((DEPTH, DECAY_LORA, R_WIDTH), 0.5 * DECAY_LORA ** -0.5)
    inp['rw_a0'] = nrm((DEPTH, R_WIDTH), 0.1)
    inp['rw_a2'] = nrm((DEPTH, AAA_LORA, R_WIDTH), 0.5 * AAA_LORA ** -0.5)
    inp['rw_g2'] = nrm((DEPTH, GATE_LORA, R_WIDTH), GATE_LORA ** -0.5)
    inp['rw_kk'] = 0.85 + nrm((DEPTH, R_WIDTH), 0.05)
    inp['rw_ka'] = 1.0 + nrm((DEPTH, R_WIDTH), 0.05)
    inp['rw_rk'] = nrm((DEPTH, R_HEADS, R_HEAD_DIM), 0.1)
    inp['rw_gn_w'] = 1.0 + nrm((DEPTH, R_WIDTH), 0.05)
    inp['rw_gn_b'] = nrm((DEPTH, R_WIDTH), 0.01)
    return inp


def reference(x_prompt, x_sample, cache_k, cache_v, page_table, state_wkv, state_shift,
              norm_gains, ff1_in, ff1_out, ff2_in, ff2_out, w_in, w_out, rel_bias_table,
              da_lq1, da_lk1, da_lq2, da_lk2, da_subln, rw_mu, rw_w0, rw_w2, rw_a0, rw_a2,
              rw_g2, rw_kk, rw_ka, rw_rk, rw_gn_w, rw_gn_b):
    P = {'norm_gains': norm_gains, 'ff1_in': ff1_in, 'ff1_out': ff1_out,
         'ff2_in': ff2_in, 'ff2_out': ff2_out, 'w_in': w_in, 'w_out': w_out,
         'da_lq1': da_lq1, 'da_lk1': da_lk1, 'da_lq2': da_lq2, 'da_lk2': da_lk2,
         'da_subln': da_subln, 'rw_mu': rw_mu, 'rw_w0': rw_w0, 'rw_w2': rw_w2,
         'rw_a0': rw_a0, 'rw_a2': rw_a2, 'rw_g2': rw_g2, 'rw_kk': rw_kk,
         'rw_ka': rw_ka, 'rw_rk': rw_rk, 'rw_gn_w': rw_gn_w, 'rw_gn_b': rw_gn_b}
    B = x_prompt.shape[0]
    hp, hs = x_prompt, x_sample
    kps, vps, kss, vss, wps, wss, sps, sss = [], [], [], [], [], [], [], []

    def attend_p(q, k, v, lam):
        return diff_attend_blocked(q, k, v, rel_bias_table, lam)

    for l in range(DEPTH):
        past_k = gather_pages(cache_k[l], page_table)
        past_v = gather_pages(cache_v[l], page_table)

        def attend_s(q, k, v, lam, past_k=past_k, past_v=past_v):
            tp, tn = past_k.shape[1], q.shape[1]
            kf = jnp.concatenate([past_k.astype(k.dtype), k], axis=1)
            vf = jnp.concatenate([past_v.astype(v.dtype), v], axis=1)
            return diff_attend(q, kf, vf, tp + jnp.arange(tn), jnp.arange(tp + tn),
                               rel_bias_table, lam)

        hp, kp, vp, wp, sp = layer(hp, l, jnp.zeros((B, R_COLS), x_prompt.dtype),
                                   jnp.zeros((B, R_HEADS, R_HEAD_DIM, R_HEAD_DIM), jnp.float32),
                                   attend_p, P)
        hs, ks_, vs_, ws_, ss_ = layer(hs, l, state_shift[l], state_wkv[l], attend_s, P)
        kps.append(kp); vps.append(vp); wps.append(wp); sps.append(sp)
        kss.append(ks_); vss.append(vs_); wss.append(ws_); sss.append(ss_)

    return (hp, hs, jnp.stack(kps), jnp.stack(vps), jnp.stack(kss), jnp.stack(vss),
            jnp.stack(wps), jnp.stack(wss), jnp.stack(sps), jnp.stack(sss))
```

```python
import functools
import math

import jax
import jax.numpy as jnp
from jax import lax
from jax.experimental import pallas as pl
from jax.experimental.pallas import tpu as pltpu

F32 = jnp.float32
BF16 = jnp.bfloat16

D_MODEL = 1024
PAGE_SIZE = 128
R_HEADS = 8
R_HEAD_DIM = 64
R_WIDTH = R_HEADS * R_HEAD_DIM
DECAY_LORA = 64
AAA_LORA = 64
GATE_LORA = 128
R_COLS = 3 * R_WIDTH + DECAY_LORA + AAA_LORA + GATE_LORA
A_HEADS = 4
A_QK_DIM = 64
A_V_DIM = 2 * A_QK_DIM
A_WIDTH = A_HEADS * A_V_DIM
QK_COLS = A_HEADS * 2 * A_QK_DIM
A_COLS = 2 * QK_COLS + A_WIDTH
PROJ_COLS = A_COLS + R_COLS
D_FF = 2816
NUM_BUCKETS = 32
MAX_DISTANCE = 128
NORM_EPS = 1e-6
SUBLN_EPS = 1e-5
GN_EPS = 64e-5
NEG_INF = -1e30

V7X_VMEM_REQUEST_BYTES = 56 * 1024 * 1024
MXU_COLS = 256
FLASH_BLOCK = 256
SCAN_CHUNK = 64
SCAN_GROUP_HEADS = 4


def _params(sem, vmem=V7X_VMEM_REQUEST_BYTES):
    return pltpu.CompilerParams(dimension_semantics=sem, vmem_limit_bytes=vmem)


def _rms(x, g):
    return x * lax.rsqrt(jnp.mean(x * x, axis=-1, keepdims=True) + NORM_EPS) * g


def _dot(a, b):
    return jnp.dot(a, b, preferred_element_type=F32)


def _dot_nt(a, b):
    return lax.dot_general(a, b, (((1,), (1,)), ((), ())), preferred_element_type=F32)


def _dot_tn(a, b):
    return lax.dot_general(a, b, (((0,), (0,)), ((), ())), preferred_element_type=F32)


def _split(x):
    hi = x.astype(BF16)
    lo = (x - hi.astype(F32)).astype(BF16)
    return hi, lo


def _seg_sum(x, seg_ones):
    hi, lo = _split(x)
    return _dot(hi, seg_ones) + _dot(lo, seg_ones)


def _ffn_kernel(x_ref, gin_ref, gout_ref, wg_ref, wu_ref, wo_ref, o_ref, xn_ref, acc_ref):
    j = pl.program_id(1)

    @pl.when(j == 0)
    def _():
        xn_ref[...] = _rms(x_ref[...], gin_ref[...]).astype(BF16)
        acc_ref[...] = jnp.zeros_like(acc_ref)

    xn = xn_ref[...]
    gate = _dot(xn, wg_ref[...])
    up = _dot(xn, wu_ref[...])
    act = (gate * jax.nn.sigmoid(gate) * up).astype(BF16)
    acc_ref[...] += _dot(act, wo_ref[...])

    @pl.when(j == pl.num_programs(1) - 1)
    def _():
        o_ref[...] = x_ref[...] + 0.5 * _rms(acc_ref[...], gout_ref[...])


def _ffn_block(x, g_in, g_out, w_in, w_out, tm, tf):
    n = x.shape[0]
    nf = D_FF // tf
    return pl.pallas_call(
        _ffn_kernel,
        out_shape=jax.ShapeDtypeStruct((n, D_MODEL), F32),
        grid=(n // tm, nf),
        in_specs=[
            pl.BlockSpec((tm, D_MODEL), lambda i, j: (i, 0)),
            pl.BlockSpec((1, D_MODEL), lambda i, j: (0, 0)),
            pl.BlockSpec((1, D_MODEL), lambda i, j: (0, 0)),
            pl.BlockSpec((D_MODEL, tf), lambda i, j: (0, j)),
            pl.BlockSpec((D_MODEL, tf), lambda i, j: (0, j + nf)),
            pl.BlockSpec((tf, D_MODEL), lambda i, j: (j, 0)),
        ],
        out_specs=pl.BlockSpec((tm, D_MODEL), lambda i, j: (i, 0)),
        scratch_shapes=[pltpu.VMEM((tm, D_MODEL), BF16), pltpu.VMEM((tm, D_MODEL), F32)],
        compiler_params=_params(("arbitrary", "arbitrary")),
    )(x, g_in, g_out, w_in, w_in, w_out)


def _proj_kernel(h_ref, g_ref, w_ref, qb_ref, kb_ref, vb_ref, k_ref, v_ref, pr_ref):
    hn = _rms(h_ref[...], g_ref[...]).astype(BF16)
    for c in range(0, PROJ_COLS, MXU_COLS):
        res = _dot(hn, w_ref[:, c:c + MXU_COLS])
        if c < QK_COLS:
            qb_ref[:, c:c + MXU_COLS] = (res * (A_QK_DIM ** -0.5)).astype(BF16)
        elif c < 2 * QK_COLS:
            k_ref[:, c - QK_COLS:c - QK_COLS + MXU_COLS] = res
            kb_ref[:, c - QK_COLS:c - QK_COLS + MXU_COLS] = res.astype(BF16)
        elif c < A_COLS:
            v_ref[:, c - 2 * QK_COLS:c - 2 * QK_COLS + MXU_COLS] = res
            vb_ref[:, c - 2 * QK_COLS:c - 2 * QK_COLS + MXU_COLS] = res.astype(BF16)
        else:
            pr_ref[:, c - A_COLS:c - A_COLS + MXU_COLS] = res


def _proj(h, g, w, tm):
    n = h.shape[0]
    row = lambda i: (i, 0)
    fixed = lambda i: (0, 0)
    return pl.pallas_call(
        _proj_kernel,
        out_shape=[
            jax.ShapeDtypeStruct((n, QK_COLS), BF16),
            jax.ShapeDtypeStruct((n, QK_COLS), BF16),
            jax.ShapeDtypeStruct((n, A_WIDTH), BF16),
            jax.ShapeDtypeStruct((n, QK_COLS), F32),
            jax.ShapeDtypeStruct((n, A_WIDTH), F32),
            jax.ShapeDtypeStruct((n, R_COLS), F32),
        ],
        grid=(n // tm,),
        in_specs=[
            pl.BlockSpec((tm, D_MODEL), row),
            pl.BlockSpec((1, D_MODEL), fixed),
            pl.BlockSpec((D_MODEL, PROJ_COLS), fixed),
        ],
        out_specs=[
            pl.BlockSpec((tm, QK_COLS), row),
            pl.BlockSpec((tm, QK_COLS), row),
            pl.BlockSpec((tm, A_WIDTH), row),
            pl.BlockSpec((tm, QK_COLS), row),
            pl.BlockSpec((tm, A_WIDTH), row),
            pl.BlockSpec((tm, R_COLS), row),
        ],
        compiler_params=_params(("arbitrary",)),
    )(h, g, w)


def _prep_kernel(pr_ref, before_ref, bnd_ref, mu_ref, w0_ref, w2_ref, a0_ref, a2_ref, g2_ref,
                 kkw_ref, ka_ref, seg_ref,
                 r_ref, ld_ref, k_ref, v_ref, kk_ref, kka_ref, g_ref, *, tm, seq_len):
    i = pl.program_id(0)
    pr = pr_ref[...]
    rows = lax.broadcasted_iota(jnp.int32, (tm, 1), 0)
    prev = pltpu.roll(pr, 1, axis=0)
    if seq_len >= tm:
        tiles_per_seq = seq_len // tm
        first = jnp.where(i % tiles_per_seq == 0, bnd_ref[0], before_ref[7:8, :])
        prev = jnp.where(rows == 0, first, prev)
    else:
        prev = jnp.where(rows % seq_len == 0, bnd_ref[...], prev)
    xr = pr + (prev - pr) * mu_ref[...]
    r = xr[:, 0:R_WIDTH]
    kr = xr[:, R_WIDTH:2 * R_WIDTH]
    vr = xr[:, 2 * R_WIDTH:3 * R_WIDTH]
    o = 3 * R_WIDTH
    wa_lo = xr[:, o:o + DECAY_LORA + AAA_LORA]
    g_lo = xr[:, o + DECAY_LORA + AAA_LORA:o + DECAY_LORA + AAA_LORA + GATE_LORA]
    lane = lax.broadcasted_iota(jnp.int32, (1, DECAY_LORA + AAA_LORA), 1)
    w_in = jnp.where(lane < DECAY_LORA, jnp.tanh(wa_lo), 0.0).astype(BF16)
    a_in = jnp.where(lane >= DECAY_LORA, wa_lo, 0.0).astype(BF16)
    log_w = -jax.nn.softplus(-(w0_ref[...] + _dot(w_in, w2_ref[...]))) - 0.5
    ld_ref[...] = -jnp.exp(log_w)
    a = jax.nn.sigmoid(a0_ref[...] + _dot(a_in, a2_ref[...]))
    g_ref[...] = _dot(jax.nn.sigmoid(g_lo).astype(BF16), g2_ref[...])
    kk = kr * kkw_ref[...]
    norm = jnp.sqrt(_seg_sum(kk * kk, seg_ref[...]))
    kk = kk / jnp.maximum(norm, 1e-12)
    r_ref[...] = r
    k_ref[...] = kr * (1.0 + (a - 1.0) * ka_ref[...])
    v_ref[...] = vr
    kk_ref[...] = kk
    kka_ref[...] = kk * a


def _prep(pr, bnd, P, seg_ones, tm, seq_len):
    n = pr.shape[0]
    row = lambda i: (i, 0)
    fixed = lambda i: (0, 0)
    if seq_len >= tm:
        tiles_per_seq = seq_len // tm
        bnd_spec = pl.BlockSpec((1, 1, R_COLS), lambda i: (i // tiles_per_seq, 0, 0))
    else:
        bnd_spec = pl.BlockSpec((tm, R_COLS), row)
    out = jax.ShapeDtypeStruct((n, R_WIDTH), F32)
    vec = pl.BlockSpec((1, R_WIDTH), fixed)
    return pl.pallas_call(
        functools.partial(_prep_kernel, tm=tm, seq_len=seq_len),
        out_shape=[out] * 7,
        grid=(n // tm,),
        in_specs=[
            pl.BlockSpec((tm, R_COLS), row),
            pl.BlockSpec((8, R_COLS), lambda i: (jnp.maximum(i * (tm // 8) - 1, 0), 0)),
            bnd_spec,
            pl.BlockSpec((1, R_COLS), fixed),
            vec,
            pl.BlockSpec((DECAY_LORA + AAA_LORA, R_WIDTH), fixed),
            vec,
            pl.BlockSpec((DECAY_LORA + AAA_LORA, R_WIDTH), fixed),
            pl.BlockSpec((GATE_LORA, R_WIDTH), fixed),
            vec,
            vec,
            pl.BlockSpec((R_WIDTH, R_WIDTH), fixed),
        ],
        out_specs=[pl.BlockSpec((tm, R_WIDTH), row)] * 7,
        compiler_params=_params(("arbitrary",)),
    )(pr, pr, bnd, P['mu'], P['w0'], P['w2'], P['a0'], P['a2'], P['g2'], P['kk'], P['ka'], seg_ones)


def _scan_kernel(r_ref, ld_ref, k_ref, v_ref, kk_ref, kka_ref, s0_ref, y_ref, sout_ref, state_ref,
                 *, chunk, group_heads, batch_block):
    c = pl.program_id(1)
    gw = group_heads * R_HEAD_DIM
    gc = group_heads * chunk
    n_groups = R_WIDTH // gw

    srow = lax.broadcasted_iota(jnp.int32, (gc, gw), 0) // chunk
    slane = lax.broadcasted_iota(jnp.int32, (gc, gw), 1) // R_HEAD_DIM
    stack_mask = srow == slane
    mrow = lax.broadcasted_iota(jnp.int32, (gc, gc), 0)
    mcol = lax.broadcasted_iota(jnp.int32, (gc, gc), 1)
    same_head = (mrow // chunk) == (mcol // chunk)
    strict = same_head & (mrow > mcol)
    incl = same_head & (mrow >= mcol)
    eye = (mrow == mcol).astype(F32)
    eye_state = (lax.broadcasted_iota(jnp.int32, (gw, gw), 0)
                 == lax.broadcasted_iota(jnp.int32, (gw, gw), 1)).astype(F32)
    rep = (lax.broadcasted_iota(jnp.int32, (R_HEAD_DIM, gw), 0)
           == lax.broadcasted_iota(jnp.int32, (R_HEAD_DIM, gw), 1) % R_HEAD_DIM).astype(F32)
    state_rows = lax.broadcasted_iota(jnp.int32, (gw, gw), 0) // R_HEAD_DIM
    state_cols = lax.broadcasted_iota(jnp.int32, (gw, gw), 1) // R_HEAD_DIM
    state_mask = state_rows == state_cols
    trow = lax.broadcasted_iota(jnp.int32, (chunk, gw), 0)

    def stack(x):
        return jnp.where(stack_mask, jnp.concatenate([x] * group_heads, axis=0), 0.0)

    def unstack(x):
        out = x[0:chunk]
        for h in range(1, group_heads):
            out = out + x[h * chunk:(h + 1) * chunk]
        return out

    for bi in range(batch_block):
        for gi in range(n_groups):
            slot = bi * n_groups + gi
            lanes = slice(gi * gw, (gi + 1) * gw)

            @pl.when(c == 0)
            def _():
                spread = lax.dot_general(s0_ref[bi, gi], rep, (((1,), (0,)), ((), ())),
                                         precision=lax.Precision.HIGHEST, preferred_element_type=F32)
                state_ref[slot] = jnp.where(state_mask, spread, 0.0)

            ld = ld_ref[bi, :, lanes]
            cum = ld
            shift = 1
            while shift < chunk:
                cum = cum + jnp.where(trow >= shift, pltpu.roll(cum, shift, axis=0), 0.0)
                shift *= 2
            w_inc = jnp.exp(cum)
            w_exc = jnp.exp(cum - ld)
            w_inv = jnp.exp(-cum)
            w_end = w_inc[chunk - 1:chunk, :]

            kk = kk_ref[bi, :, lanes]
            xa = stack(-kk * w_exc).astype(BF16)
            xr = stack(r_ref[bi, :, lanes] * w_inc)
            xr_b = xr.astype(BF16)
            zb = stack(kka_ref[bi, :, lanes] * w_inv).astype(BF16)
            zk = stack(k_ref[bi, :, lanes] * w_inv).astype(BF16)
            vs = stack(v_ref[bi, :, lanes])
            vs_b = vs.astype(BF16)

            nmat = jnp.where(strict, _dot_nt(xa, zb), 0.0)
            lak = jnp.where(strict, _dot_nt(xa, zk), 0.0).astype(BF16)
            mrb = jnp.where(incl, _dot_nt(xr_b, zb), 0.0).astype(BF16)
            mrk = jnp.where(incl, _dot_nt(xr_b, zk), 0.0).astype(BF16)

            tmat = eye + nmat
            npow = nmat
            span = 2
            while span < chunk:
                npow_b = npow.astype(BF16)
                npow = _dot(npow_b, npow_b)
                tmat = tmat + _dot(tmat.astype(BF16), npow.astype(BF16))
                span *= 2
            tmat_b = tmat.astype(BF16)

            p = _dot(tmat_b, xa)
            q = _dot(tmat_b, _dot(lak, vs_b).astype(BF16))
            p_b = p.astype(BF16)
            q_b = q.astype(BF16)
            gmat = (eye_state + _dot_tn(p_b, zb)) * w_end
            hmat = (_dot_tn(q_b, zb) + _dot_tn(vs_b, zk)) * w_end
            yp = (xr + _dot(mrb, p_b)).astype(BF16)
            yq = _dot(mrb, q_b) + _dot(mrk, vs_b)

            s_old = state_ref[slot]
            s_old_b = s_old.astype(BF16)
            y_ref[bi, :, lanes] = unstack(_dot_nt(yp, s_old_b) + yq)
            s_new = _dot(s_old_b, gmat.astype(BF16)) + hmat
            state_ref[slot] = s_new

            @pl.when(c == pl.num_programs(1) - 1)
            def _():
                sout_ref[bi, gi] = lax.dot_general(s_new, rep, (((1,), (1,)), ((), ())),
                                                   precision=lax.Precision.HIGHEST, preferred_element_type=F32)


def _wkv_scan(r, ld, k, v, kk, kka, s0, chunk, group_heads, batch_block):
    b, t, _ = r.shape
    gw = group_heads * R_HEAD_DIM
    n_groups = R_WIDTH // gw
    s0g = s0.reshape(b, n_groups, gw, R_HEAD_DIM)
    tok = pl.BlockSpec((batch_block, chunk, R_WIDTH), lambda i, c: (i, c, 0))
    st = pl.BlockSpec((batch_block, n_groups, gw, R_HEAD_DIM), lambda i, c: (i, 0, 0, 0))
    y, s_out = pl.pallas_call(
        functools.partial(_scan_kernel, chunk=chunk, group_heads=group_heads, batch_block=batch_block),
        out_shape=[jax.ShapeDtypeStruct((b, t, R_WIDTH), F32),
                   jax.ShapeDtypeStruct((b, n_groups, gw, R_HEAD_DIM), F32)],
        grid=(b // batch_block, t // chunk),
        in_specs=[tok] * 6 + [st],
        out_specs=[tok, st],
        scratch_shapes=[pltpu.VMEM((batch_block * n_groups, gw, gw), F32)],
        compiler_params=_params(("arbitrary", "arbitrary")),
    )(r, ld, k, v, kk, kka, s0g)
    return y, s_out.reshape(b, R_HEADS, R_HEAD_DIM, R_HEAD_DIM)


def _bias_from_distance(n, table_ref, head):
    n = jnp.maximum(n, 0)
    max_exact = NUM_BUCKETS // 2
    large = max_exact + (jnp.log(jnp.maximum(n, 1).astype(F32) / max_exact)
                         / math.log(MAX_DISTANCE / max_exact)
                         * (NUM_BUCKETS - max_exact)).astype(jnp.int32)
    large = jnp.minimum(large, NUM_BUCKETS - 1)
    bucket = jnp.where(n < max_exact, n, large)
    bias = jnp.zeros(n.shape, F32)
    for b in range(NUM_BUCKETS):
        bias = jnp.where(bucket == b, table_ref[b, head], bias)
    return bias


def _lambda(lq1_ref, lk1_ref, lq2_ref, lk2_ref, lam_init):
    s1 = jnp.sum(lq1_ref[...] * lk1_ref[...], axis=-1, keepdims=True)
    s2 = jnp.sum(lq2_ref[...] * lk2_ref[...], axis=-1, keepdims=True)
    return jnp.exp(s1) - jnp.exp(s2) + lam_init


def _subln(o, g_ref, lam_init):
    o = o * lax.rsqrt(jnp.mean(o * o, axis=-1, keepdims=True) + SUBLN_EPS) * g_ref[...]
    return o * (1.0 - lam_init)


def _flash_kernel(table_ref, q_ref, k_ref, v_ref, lq1_ref, lk1_ref, lq2_ref, lk2_ref, g_ref,
                  o_ref, bias_ref, *, lam_init):
    b = pl.program_id(0)
    h = pl.program_id(1)
    qi = pl.program_id(2)
    blk = FLASH_BLOCK

    @pl.when((b == 0) & (qi == 0))
    def _():
        di = lax.broadcasted_iota(jnp.int32, (blk, blk), 0) - lax.broadcasted_iota(jnp.int32, (blk, blk), 1)
        bias_ref[h, 0] = jnp.where(di >= 0, _bias_from_distance(di, table_ref, h), NEG_INF)
        bias_ref[h, 1] = _bias_from_distance(di + blk, table_ref, h)
        bias_ref[h, 2] = _bias_from_distance(di + 2 * blk, table_ref, h)

    q = q_ref[0]
    lane = lax.broadcasted_iota(jnp.int32, (1, A_V_DIM), 1)
    q0 = jnp.where(lane < A_QK_DIM, q, 0).astype(BF16)
    q1 = jnp.where(lane >= A_QK_DIM, q, 0).astype(BF16)

    def stream(qm, kc, vc, bias, carry):
        m, l, acc = carry
        s = _dot_nt(qm, kc) + bias
        m_new = jnp.maximum(m, jnp.max(s, axis=-1, keepdims=True))
        alpha = jnp.exp(m - m_new)
        p = jnp.exp(s - m_new)
        l = l * alpha + jnp.sum(p, axis=-1, keepdims=True)
        acc = acc * alpha + _dot(p.astype(BF16), vc)
        return m_new, l, acc

    def body(j, carry):
        c0, c1 = carry
        kc = k_ref[0, pl.ds(pl.multiple_of(j * blk, blk), blk), :]
        vc = v_ref[0, pl.ds(pl.multiple_of(j * blk, blk), blk), :]
        bias = bias_ref[h, jnp.minimum(qi - j, 2)]
        return stream(q0, kc, vc, bias, c0), stream(q1, kc, vc, bias, c1)

    init = (jnp.full((blk, 1), NEG_INF, F32), jnp.zeros((blk, 1), F32), jnp.zeros((blk, A_V_DIM), F32))
    (_, l0, acc0), (_, l1, acc1) = lax.fori_loop(0, qi + 1, body, (init, init))
    lam = _lambda(lq1_ref, lk1_ref, lq2_ref, lk2_ref, lam_init)
    o = acc0 / l0 - lam * (acc1 / l1)
    o_ref[0] = _subln(o, g_ref, lam_init)


def _flash_attention(qb, kb, vb, table, lam_vecs, subln_g, lam_init):
    b, t, _ = qb.shape
    blk = FLASH_BLOCK
    vec = pl.BlockSpec((1, A_QK_DIM), lambda bb, h, i: (0, 0))
    return pl.pallas_call(
        functools.partial(_flash_kernel, lam_init=lam_init),
        out_shape=jax.ShapeDtypeStruct((b, t, A_WIDTH), F32),
        grid=(b, A_HEADS, t // blk),
        in_specs=[
            pl.BlockSpec(memory_space=pltpu.SMEM),
            pl.BlockSpec((1, blk, A_V_DIM), lambda bb, h, i: (bb, i, h)),
            pl.BlockSpec((1, t, A_V_DIM), lambda bb, h, i: (bb, 0, h)),
            pl.BlockSpec((1, t, A_V_DIM), lambda bb, h, i: (bb, 0, h)),
            vec, vec, vec, vec,
            pl.BlockSpec((1, A_V_DIM), lambda bb, h, i: (0, 0)),
        ],
        out_specs=pl.BlockSpec((1, blk, A_V_DIM), lambda bb, h, i: (bb, i, h)),
        scratch_shapes=[pltpu.VMEM((A_HEADS, 3, blk, blk), F32)],
        compiler_params=_params(("arbitrary", "arbitrary", "arbitrary")),
    )(table, qb, kb, vb, *lam_vecs, subln_g)


def _decode_kernel(pt_ref, table_ref, q_ref, kn_ref, vn_ref, lq1_ref, lk1_ref, lq2_ref, lk2_ref, g_ref, *rest,
                   n_pages, n_new, lam_init):
    k_pages = rest[:n_pages]
    v_pages = rest[n_pages:2 * n_pages]
    o_ref = rest[2 * n_pages]
    bias_ref, bias_new_ref = rest[2 * n_pages + 1:]
    del pt_ref
    rows = 2 * A_HEADS * n_new
    past = n_pages * PAGE_SIZE

    @pl.when(pl.program_id(0) == 0)
    def _():
        for h in range(A_HEADS):
            tok = lax.broadcasted_iota(jnp.int32, (n_new, past), 0)
            pos = lax.broadcasted_iota(jnp.int32, (n_new, past), 1)
            bp = _bias_from_distance(past + tok - pos, table_ref, h)
            tok = lax.broadcasted_iota(jnp.int32, (n_new, PAGE_SIZE), 0)
            pos = lax.broadcasted_iota(jnp.int32, (n_new, PAGE_SIZE), 1)
            bn = jnp.where(pos <= tok, _bias_from_distance(tok - pos, table_ref, h), NEG_INF)
            for m in range(2):
                r0 = (m * A_HEADS + h) * n_new
                bias_ref[r0:r0 + n_new, :] = bp
                bias_new_ref[r0:r0 + n_new, :] = bn

    rr = lax.broadcasted_iota(jnp.int32, (rows, QK_COLS), 0)
    ll = lax.broadcasted_iota(jnp.int32, (rows, QK_COLS), 1)
    q_mask = (ll // A_V_DIM == (rr // n_new) % A_HEADS) & ((ll // A_QK_DIM) % 2 == rr // (A_HEADS * n_new))
    qf = jnp.where(q_mask, q_ref[0], 0.0)
    qb = qf.astype(BF16)

    s_past = jnp.concatenate([_dot_nt(qb, kp[0].astype(BF16)) for kp in k_pages], axis=1) + bias_ref[...]
    kn = kn_ref[0]
    vn = vn_ref[0]
    s_new = [jnp.sum(qf * kn[t:t + 1, :], axis=-1, keepdims=True) + bias_new_ref[:, t:t + 1] for t in range(n_new)]
    m = jnp.max(s_past, axis=-1, keepdims=True)
    for s in s_new:
        m = jnp.maximum(m, s)
    p_past = jnp.exp(s_past - m)
    p_new = [jnp.exp(s - m) for s in s_new]
    l = jnp.sum(p_past, axis=-1, keepdims=True)
    for p in p_new:
        l = l + p
    p_b = p_past.astype(BF16)
    acc = jnp.zeros((rows, A_WIDTH), F32)
    for j, vp in enumerate(v_pages):
        acc = acc + _dot(p_b[:, j * PAGE_SIZE:(j + 1) * PAGE_SIZE], vp[0].astype(BF16))
    for t in range(n_new):
        acc = acc + p_new[t] * vn[t:t + 1, :]
    acc = acc / l
    half = A_HEADS * n_new
    lam = _lambda(lq1_ref, lk1_ref, lq2_ref, lk2_ref, lam_init)
    comb = acc[0:half] - lam * acc[half:rows]
    for h in range(A_HEADS):
        o = comb[h * n_new:(h + 1) * n_new, h * A_V_DIM:(h + 1) * A_V_DIM]
        o_ref[0, :, h * A_V_DIM:(h + 1) * A_V_DIM] = _subln(o, g_ref, lam_init)


def _decode_attention(q, k_new, v_new, cache_k, cache_v, page_table, table, lam_vecs, subln_g, lam_init):
    b, n_new, _ = q.shape
    n_pages = page_table.shape[1]
    rows = 2 * A_HEADS * n_new
    q_rows = jnp.tile(q, (1, 2 * A_HEADS, 1))
    seq = lambda i, pt: (i, 0, 0)
    fixed = lambda i, pt: (0, 0)

    def page_spec(j):
        return pl.BlockSpec((1, PAGE_SIZE, QK_COLS), lambda i, pt: (pt[i, j], 0, 0))

    vec = pl.BlockSpec((1, A_QK_DIM), fixed)
    grid_spec = pltpu.PrefetchScalarGridSpec(
        num_scalar_prefetch=1,
        grid=(b,),
        in_specs=[
            pl.BlockSpec(memory_space=pltpu.SMEM),
            pl.BlockSpec((1, rows, QK_COLS), seq),
            pl.BlockSpec((1, n_new, QK_COLS), seq),
            pl.BlockSpec((1, n_new, A_WIDTH), seq),
            vec, vec, vec, vec,
            pl.BlockSpec((1, A_V_DIM), fixed),
        ] + [page_spec(j) for j in range(n_pages)] * 2,
        out_specs=pl.BlockSpec((1, n_new, A_WIDTH), seq),
        scratch_shapes=[pltpu.VMEM((rows, n_pages * PAGE_SIZE), F32), pltpu.VMEM((rows, PAGE_SIZE), F32)],
    )
    return pl.pallas_call(
        functools.partial(_decode_kernel, n_pages=n_pages, n_new=n_new, lam_init=lam_init),
        out_shape=jax.ShapeDtypeStruct((b, n_new, A_WIDTH), F32),
        grid_spec=grid_spec,
        compiler_params=_params(("arbitrary",)),
    )(page_table, table, q_rows, k_new, v_new, *lam_vecs, subln_g,
      *([cache_k] * n_pages), *([cache_v] * n_pages))


def _merge_kernel(y_ref, r_ref, k_ref, v_ref, g_ref, oa_ref, h_ref, gnw_ref, gnb_ref, rk_ref, seg_ref,
                  wo_ref, gain_ref, out_ref):
    seg = seg_ref[...]
    y = y_ref[...]
    mu = _seg_sum(y, seg) * (1.0 / R_HEAD_DIM)
    yc = y - mu
    var = _seg_sum(yc * yc, seg) * (1.0 / R_HEAD_DIM)
    yn = yc * lax.rsqrt(var + GN_EPS) * gnw_ref[...] + gnb_ref[...]
    bonus = _seg_sum(r_ref[...] * k_ref[...] * rk_ref[...], seg) * v_ref[...]
    o_rwkv = ((yn + bonus) * g_ref[...]).astype(BF16)
    m = _dot(o_rwkv, wo_ref[0:R_WIDTH, :]) + _dot(oa_ref[...].astype(BF16), wo_ref[R_WIDTH:R_WIDTH + A_WIDTH, :])
    out_ref[...] = h_ref[...] + _rms(m, gain_ref[...])


def _merge(y, r, k, v, g, o_attn, h, P, seg_ones, w_out, gain, tm):
    n = y.shape[0]
    row = lambda i: (i, 0)
    fixed = lambda i: (0, 0)
    tok = pl.BlockSpec((tm, R_WIDTH), row)
    vec = pl.BlockSpec((1, R_WIDTH), fixed)
    return pl.pallas_call(
        _merge_kernel,
        out_shape=jax.ShapeDtypeStruct((n, D_MODEL), F32),
        grid=(n // tm,),
        in_specs=[tok, tok, tok, tok, tok, tok,
                  pl.BlockSpec((tm, D_MODEL), row),
                  vec, vec, vec,
                  pl.BlockSpec((R_WIDTH, R_WIDTH), fixed),
                  pl.BlockSpec((R_WIDTH + A_WIDTH, D_MODEL), fixed),
                  pl.BlockSpec((1, D_MODEL), fixed)],
        out_specs=pl.BlockSpec((tm, D_MODEL), row),
        compiler_params=_params(("arbitrary",)),
    )(y, r, k, v, g, o_attn, h, P['gn_w'], P['gn_b'], P['rk'], seg_ones, w_out, gain)


def _layer(x, shift0, wkv0, attend, W, *, tm, tf, chunk, group_heads, batch_block):
    b, t, _ = x.shape
    n = b * t
    g = W['gains']
    h = _ffn_block(x.reshape(n, D_MODEL), g[0:1], g[1:2], W['ff1_in'], W['ff1_out'], tm, tf)
    qb, kb, vb, k, v, pr = _proj(h, g[2:3], W['w_in'], tm)
    if t >= tm:
        bnd = shift0.reshape(b, 1, R_COLS)
    else:
        bnd = jnp.repeat(shift0, t, axis=0)
    r, ld, kr, vr, kk, kka, gate = _prep(pr, bnd, W['rw'], W['seg_ones'], tm, t)
    o_attn = attend(qb, kb, vb, k, v).reshape(n, A_WIDTH)

    def seq(a):
        return a.reshape(b, t, R_WIDTH)

    pad = (-t) % chunk
    scan_in = [seq(a) for a in (r, ld, kr, vr, kk, kka)]
    if pad:
        scan_in = [jnp.pad(a, ((0, 0), (0, pad), (0, 0))) for a in scan_in]
    y, wkv = _wkv_scan(*scan_in, wkv0, chunk, group_heads, batch_block)
    y = y[:, :t].reshape(n, R_WIDTH)
    h = _merge(y, r, kr, vr, gate, o_attn, h, W['rw'], W['seg_ones'], W['w_out'], g[3:4], tm)
    h = _ffn_block(h, g[4:5], g[5:6], W['ff2_in'], W['ff2_out'], tm, tf)
    shift = pr.reshape(b, t, R_COLS)[:, -1]
    return (h.reshape(b, t, D_MODEL), k.reshape(b, t, A_HEADS, 2, A_QK_DIM), v.reshape(b, t, A_HEADS, A_V_DIM),
            wkv, shift)


def kernel(x_prompt, x_sample, cache_k, cache_v, page_table, state_wkv, state_shift, norm_gains, ff1_in, ff1_out, ff2_in, ff2_out, w_in, w_out, rel_bias_table, da_lq1, da_lk1, da_lq2, da_lk2, da_subln, rw_mu, rw_w0, rw_w2, rw_a0, rw_a2, rw_g2, rw_kk, rw_ka, rw_rk, rw_gn_w, rw_gn_b):
    depth = norm_gains.shape[0]
    b, t, _ = x_prompt.shape
    bs, ts, _ = x_sample.shape
    n_pool = cache_k.shape[1]
    lane_head = jnp.arange(R_WIDTH) // R_HEAD_DIM
    seg_ones = (lane_head[:, None] == lane_head[None, :]).astype(BF16)
    lora_pad = jnp.zeros((DECAY_LORA, R_WIDTH), BF16)

    hp, hs = x_prompt, x_sample
    outs = [[] for _ in range(8)]
    for l in range(depth):
        lam_init = 0.8 - 0.6 * math.exp(-0.3 * l)
        W = {
            'gains': norm_gains[l],
            'ff1_in': ff1_in[l].astype(BF16), 'ff1_out': ff1_out[l].astype(BF16),
            'ff2_in': ff2_in[l].astype(BF16), 'ff2_out': ff2_out[l].astype(BF16),
            'w_in': w_in[l].astype(BF16), 'w_out': w_out[l].astype(BF16),
            'seg_ones': seg_ones,
            'rw': {
                'mu': rw_mu[l][None], 'w0': rw_w0[l][None], 'a0': rw_a0[l][None],
                'w2': jnp.concatenate([rw_w2[l].astype(BF16), lora_pad], axis=0),
                'a2': jnp.concatenate([lora_pad, rw_a2[l].astype(BF16)], axis=0),
                'g2': rw_g2[l].astype(BF16),
                'kk': rw_kk[l][None], 'ka': rw_ka[l][None], 'rk': rw_rk[l].reshape(1, R_WIDTH),
                'gn_w': rw_gn_w[l][None], 'gn_b': rw_gn_b[l][None],
            },
        }
        lam_vecs = (da_lq1[l][None], da_lk1[l][None], da_lq2[l][None], da_lk2[l][None])
        subln_g = da_subln[l][None]
        pool_k = cache_k[l].reshape(n_pool, PAGE_SIZE, QK_COLS)
        pool_v = cache_v[l].reshape(n_pool, PAGE_SIZE, A_WIDTH)

        def attend_p(qb, kb, vb, k, v):
            del k, v
            three = lambda a: a.reshape(b, t, QK_COLS)
            return _flash_attention(three(qb), three(kb), three(vb), rel_bias_table, lam_vecs, subln_g, lam_init)

        def attend_s(qb, kb, vb, k, v):
            del kb, vb
            three = lambda a: a.reshape(bs, ts, QK_COLS)
            return _decode_attention(three(qb).astype(F32), three(k), three(v), pool_k, pool_v, page_table,
                                     rel_bias_table, lam_vecs, subln_g, lam_init)

        hp, kp, vp, wp, sp = _layer(hp, jnp.zeros((b, R_COLS), F32),
                                    jnp.zeros((b, R_HEADS, R_HEAD_DIM, R_HEAD_DIM), F32), attend_p, W,
                                    tm=512, tf=D_FF // 2, chunk=SCAN_CHUNK, group_heads=SCAN_GROUP_HEADS,
                                    batch_block=2)
        hs, ks, vs, ws, ss = _layer(hs, state_shift[l], state_wkv[l], attend_s, W,
                                    tm=bs * ts, tf=D_FF // 2, chunk=8, group_heads=R_HEADS, batch_block=8)
        for lst, val in zip(outs, (kp, vp, ks, vs, wp, ws, sp, ss)):
            lst.append(val)

    return (hp, hs) + tuple(jnp.stack(o) for o in outs)
```

```python
import functools
import math

import jax
import jax.numpy as jnp
from jax import lax
from jax.experimental import pallas as pl
from jax.experimental.pallas import tpu as pltpu

F32 = jnp.float32
BF16 = jnp.bfloat16

D_MODEL = 1024
PAGE_SIZE = 128
R_HEADS = 8
R_HEAD_DIM = 64
R_WIDTH = R_HEADS * R_HEAD_DIM
DECAY_LORA = 64
AAA_LORA = 64
GATE_LORA = 128
R_COLS = 3 * R_WIDTH + DECAY_LORA + AAA_LORA + GATE_LORA
A_HEADS = 4
A_QK_DIM = 64
A_V_DIM = 2 * A_QK_DIM
A_WIDTH = A_HEADS * A_V_DIM
QK_COLS = A_HEADS * 2 * A_QK_DIM
A_COLS = 2 * QK_COLS + A_WIDTH
PROJ_COLS = A_COLS + R_COLS
D_FF = 2816
NUM_BUCKETS = 32
MAX_DISTANCE = 128
NORM_EPS = 1e-6
SUBLN_EPS = 1e-5
GN_EPS = 64e-5
NEG_INF = -1e30

V7X_VMEM_REQUEST_BYTES = 56 * 1024 * 1024
MXU_COLS = 256
FLASH_BLOCK = 512
FLASH_STRIP_ROWS = 32
BIAS_SUB_TILE = 128
LAST_BUCKET_FROM = MAX_DISTANCE
assert FLASH_BLOCK >= MAX_DISTANCE and FLASH_BLOCK % BIAS_SUB_TILE == 0
DECODE_ROWS = 8
CARRY_ROWS = 8
SCAN_CHUNK = 64
SCAN_GROUP_HEADS = 2


def _params(sem, vmem=V7X_VMEM_REQUEST_BYTES):
    return pltpu.CompilerParams(dimension_semantics=sem, vmem_limit_bytes=vmem)


def _rms(x, g):
    return x * lax.rsqrt(jnp.mean(x * x, axis=-1, keepdims=True) + NORM_EPS) * g


def _dot(a, b):
    return jnp.dot(a, b, preferred_element_type=F32)


def _dot_nt(a, b):
    return lax.dot_general(a, b, (((1,), (1,)), ((), ())), preferred_element_type=F32)


def _dot_tn(a, b):
    return lax.dot_general(a, b, (((0,), (0,)), ((), ())), preferred_element_type=F32)


def _split(x):
    hi = x.astype(BF16)
    lo = (x - hi.astype(F32)).astype(BF16)
    return hi, lo


def _seg_sum(x, seg_ones):
    hi, lo = _split(x)
    return _dot(hi, seg_ones) + _dot(lo, seg_ones)


def _ffn_math(x, gin_ref, gout_ref, win_ref, wo_ref, tf):
    xn = _rms(x, gin_ref[...]).astype(BF16)
    acc = None
    for f in range(0, D_FF, tf):
        gate = _dot(xn, win_ref[:, f:f + tf])
        up = _dot(xn, win_ref[:, D_FF + f:D_FF + f + tf])
        act = (gate * jax.nn.sigmoid(gate) * up).astype(BF16)
        part = _dot(act, wo_ref[f:f + tf, :])
        acc = part if acc is None else acc + part
    return x + 0.5 * _rms(acc, gout_ref[...])


def _ffn_kernel(x_ref, gin_ref, gout_ref, win_ref, wo_ref, o_ref, *, tf):
    o_ref[...] = _ffn_math(x_ref[...], gin_ref, gout_ref, win_ref, wo_ref, tf)


def _ffn_block(x, g_in, g_out, w_in, w_out, tm, tf):
    n = x.shape[0]
    fixed = lambda i: (0, 0)
    resident = pl.Buffered(1)
    return pl.pallas_call(
        functools.partial(_ffn_kernel, tf=tf),
        out_shape=jax.ShapeDtypeStruct((n, D_MODEL), F32),
        grid=(n // tm,),
        in_specs=[
            pl.BlockSpec((tm, D_MODEL), lambda i: (i, 0)),
            pl.BlockSpec((1, D_MODEL), fixed),
            pl.BlockSpec((1, D_MODEL), fixed),
            pl.BlockSpec((D_MODEL, 2 * D_FF), fixed, pipeline_mode=resident),
            pl.BlockSpec((D_FF, D_MODEL), fixed, pipeline_mode=resident),
        ],
        out_specs=pl.BlockSpec((tm, D_MODEL), lambda i: (i, 0)),
        compiler_params=_params(("arbitrary",)),
    )(x, g_in, g_out, w_in, w_out)


def _proj_kernel(h_ref, g_ref, w_ref, wt_ref, bnd_ref, mu_ref, w0_ref, w2_ref, a0_ref, a2_ref, g2_ref,
                 kkw_ref, ka_ref, seg_ref,
                 qb_ref, kb_ref, vb_ref, k_ref, v_ref, tail_ref,
                 r_ref, ld_ref, kr_ref, vr_ref, kk_ref, kka_ref, gate_ref, carry_ref, *, tm, seq_len, transposed):
    i = pl.program_id(0)
    hn = _rms(h_ref[...], g_ref[...]).astype(BF16)
    pr = jnp.concatenate([_dot(hn, w_ref[:, c:c + MXU_COLS]) for c in range(A_COLS, PROJ_COLS, MXU_COLS)], axis=1)
    tail_ref[0] = pr[tm - tail_ref.shape[1]:tm, :]
    rows = lax.broadcasted_iota(jnp.int32, (tm, 1), 0)
    prev = pltpu.roll(pr, 1, axis=0)
    if seq_len >= tm:
        tiles_per_seq = seq_len // tm
        first = jnp.where(i % tiles_per_seq == 0, bnd_ref[0], carry_ref[CARRY_ROWS - 1:CARRY_ROWS, :])
        prev = jnp.where(rows == 0, first, prev)
        carry_ref[...] = pr[tm - CARRY_ROWS:tm, :]
    else:
        prev = jnp.where(rows % seq_len == 0, bnd_ref[...], prev)
    xr = pr + (prev - pr) * mu_ref[...]
    r = xr[:, 0:R_WIDTH]
    kr = xr[:, R_WIDTH:2 * R_WIDTH]
    vr = xr[:, 2 * R_WIDTH:3 * R_WIDTH]
    o = 3 * R_WIDTH
    wa_lo = xr[:, o:o + DECAY_LORA + AAA_LORA]
    g_lo = xr[:, o + DECAY_LORA + AAA_LORA:o + DECAY_LORA + AAA_LORA + GATE_LORA]
    lane = lax.broadcasted_iota(jnp.int32, (1, DECAY_LORA + AAA_LORA), 1)
    w_in = jnp.where(lane < DECAY_LORA, jnp.tanh(wa_lo), 0.0).astype(BF16)
    a_in = jnp.where(lane >= DECAY_LORA, wa_lo, 0.0).astype(BF16)
    g_in = jax.nn.sigmoid(g_lo).astype(BF16)
    kk = kr * kkw_ref[...]
    kk_sq_hi, kk_sq_lo = _split(kk * kk)
    r_ref[...] = r
    vr_ref[...] = vr

    q_scale = A_QK_DIM ** -0.5
    if transposed:
        qb_ref[0] = (_dot_nt(wt_ref[0:QK_COLS, :], hn) * q_scale).astype(BF16)
    for c in range(0, A_COLS, MXU_COLS):
        if transposed and c < QK_COLS:
            continue
        res = _dot(hn, w_ref[:, c:c + MXU_COLS])
        if c < QK_COLS:
            qb_ref[:, c:c + MXU_COLS] = (res * q_scale).astype(BF16)
        elif c < 2 * QK_COLS:
            kb_ref[:, c - QK_COLS:c - QK_COLS + MXU_COLS] = res.astype(BF16)
            if transposed:
                k_ref[0, c - QK_COLS:c - QK_COLS + MXU_COLS, :] = res.T
            else:
                k_ref[:, c - QK_COLS:c - QK_COLS + MXU_COLS] = res
        else:
            for s in range(0, MXU_COLS, A_V_DIM):
                v_ref[:, (c - 2 * QK_COLS + s) // A_V_DIM, :] = res[:, s:s + A_V_DIM]
            if transposed:
                vb_ref[0, c - 2 * QK_COLS:c - 2 * QK_COLS + MXU_COLS, :] = res.T.astype(BF16)
            else:
                vb_ref[:, c - 2 * QK_COLS:c - 2 * QK_COLS + MXU_COLS] = res.astype(BF16)

    log_w = -jax.nn.softplus(-(w0_ref[...] + _dot(w_in, w2_ref[...]))) - 0.5
    ld_ref[...] = -jnp.exp(log_w)
    a = jax.nn.sigmoid(a0_ref[...] + _dot(a_in, a2_ref[...]))
    gate_ref[...] = _dot(g_in, g2_ref[...])
    norm = jnp.sqrt(_dot(kk_sq_hi, seg_ref[...]) + _dot(kk_sq_lo, seg_ref[...]))
    kk = kk / jnp.maximum(norm, 1e-12)
    kr_ref[...] = kr * (1.0 + (a - 1.0) * ka_ref[...])
    kk_ref[...] = kk
    kka_ref[...] = kk * a


def _proj(h, g, w, wt, bnd, P, seg_ones, tm, seq_len, transposed):
    n = h.shape[0]
    row = lambda i: (i, 0)
    fixed = lambda i: (0, 0)
    rm_shape, rm_spec = (n, QK_COLS), pl.BlockSpec((tm, QK_COLS), row)
    if transposed:
        tiles_per_seq = seq_len // tm
        t_shape = (n // seq_len, QK_COLS, seq_len)
        t_spec = pl.BlockSpec((1, QK_COLS, tm), lambda i: (i // tiles_per_seq, 0, i % tiles_per_seq))
    else:
        t_shape, t_spec = rm_shape, rm_spec
    if seq_len >= tm:
        tiles_per_seq = seq_len // tm
        bnd_spec = pl.BlockSpec((1, 1, R_COLS), lambda i: (i // tiles_per_seq, 0, 0))
        tail_shape = (n // seq_len, CARRY_ROWS, R_COLS)
        tail_spec = pl.BlockSpec((1, CARRY_ROWS, R_COLS), lambda i: (i // tiles_per_seq, 0, 0))
    else:
        bnd_spec = pl.BlockSpec((tm, R_COLS), row)
        tail_shape = (n // tm, tm, R_COLS)
        tail_spec = pl.BlockSpec((1, tm, R_COLS), lambda i: (i, 0, 0))
    heads = jax.ShapeDtypeStruct((n, R_WIDTH), F32)
    vec = pl.BlockSpec((1, R_WIDTH), fixed)
    lora = pl.BlockSpec((DECAY_LORA + AAA_LORA, R_WIDTH), fixed)
    return pl.pallas_call(
        functools.partial(_proj_kernel, tm=tm, seq_len=seq_len, transposed=transposed),
        out_shape=[
            jax.ShapeDtypeStruct(t_shape, BF16),
            jax.ShapeDtypeStruct(rm_shape, BF16),
            jax.ShapeDtypeStruct(t_shape, BF16),
            jax.ShapeDtypeStruct(t_shape, F32),
            jax.ShapeDtypeStruct((n, A_HEADS, A_V_DIM), F32),
            jax.ShapeDtypeStruct(tail_shape, F32),
        ] + [heads] * 7,
        grid=(n // tm,),
        in_specs=[
            pl.BlockSpec((tm, D_MODEL), row),
            pl.BlockSpec((1, D_MODEL), fixed),
            pl.BlockSpec((D_MODEL, PROJ_COLS), fixed),
            pl.BlockSpec((QK_COLS, D_MODEL), fixed),
            bnd_spec,
            pl.BlockSpec((1, R_COLS), fixed),
            vec, lora, vec, lora,
            pl.BlockSpec((GATE_LORA, R_WIDTH), fixed),
            vec, vec,
            pl.BlockSpec((R_WIDTH, R_WIDTH), fixed),
        ],
        out_specs=[t_spec, rm_spec, t_spec, t_spec, pl.BlockSpec((tm, A_HEADS, A_V_DIM), lambda i: (i, 0, 0)),
                   tail_spec] + [pl.BlockSpec((tm, R_WIDTH), row)] * 7,
        scratch_shapes=[pltpu.VMEM((CARRY_ROWS, R_COLS), F32)],
        compiler_params=_params(("arbitrary",)),
    )(h, g, w, wt, bnd, P['mu'], P['w0'], P['w2'], P['a0'], P['a2'], P['g2'], P['kk'], P['ka'], seg_ones)


def _scan_kernel(r_ref, ld_ref, k_ref, v_ref, kk_ref, kka_ref, s0_ref, y_ref, sout_ref, state_ref,
                 *, chunk, group_heads, batch_block):
    c = pl.program_id(1)
    gw = group_heads * R_HEAD_DIM
    gc = group_heads * chunk
    n_groups = R_WIDTH // gw

    srow = lax.broadcasted_iota(jnp.int32, (gc, gw), 0) // chunk
    slane = lax.broadcasted_iota(jnp.int32, (gc, gw), 1) // R_HEAD_DIM
    stack_mask = srow == slane
    mrow = lax.broadcasted_iota(jnp.int32, (gc, gc), 0)
    mcol = lax.broadcasted_iota(jnp.int32, (gc, gc), 1)
    same_head = (mrow // chunk) == (mcol // chunk)
    strict = same_head & (mrow > mcol)
    incl = same_head & (mrow >= mcol)
    eye = (mrow == mcol).astype(F32)
    eye_state = (lax.broadcasted_iota(jnp.int32, (gw, gw), 0)
                 == lax.broadcasted_iota(jnp.int32, (gw, gw), 1)).astype(F32)
    state_rows = lax.broadcasted_iota(jnp.int32, (gw, gw), 0) // R_HEAD_DIM
    state_cols = lax.broadcasted_iota(jnp.int32, (gw, gw), 1) // R_HEAD_DIM
    state_mask = state_rows == state_cols
    trow = lax.broadcasted_iota(jnp.int32, (chunk, gw), 0)

    def stack(x):
        return jnp.where(stack_mask, jnp.concatenate([x] * group_heads, axis=0), 0.0)

    def unstack(x):
        out = x[0:chunk]
        for h in range(1, group_heads):
            out = out + x[h * chunk:(h + 1) * chunk]
        return out

    slots = [(bi, gi) for bi in range(batch_block) for gi in range(n_groups)]

    @pl.when(c == 0)
    def _():
        for slot, (bi, gi) in enumerate(slots):
            spread = jnp.concatenate([s0_ref[bi, gi]] * group_heads, axis=1)
            state_ref[slot] = jnp.where(state_mask, spread, 0.0)

    def lanes_of(ref, s):
        bi, gi = slots[s]
        return ref[bi, :, gi * gw:(gi + 1) * gw]

    every = range(len(slots))
    ld = [lanes_of(ld_ref, s) for s in every]
    cum = list(ld)
    shift = 1
    while shift < chunk:
        cum = [x + jnp.where(trow >= shift, pltpu.roll(x, shift, axis=0), 0.0) for x in cum]
        shift *= 2
    w_inc = [jnp.exp(x) for x in cum]
    w_exc = [jnp.exp(x - d) for x, d in zip(cum, ld)]
    w_inv = [jnp.exp(-x) for x in cum]
    w_end = [x[chunk - 1:chunk, :] for x in w_inc]

    xa = [stack(-lanes_of(kk_ref, s) * w_exc[s]).astype(BF16) for s in every]
    xr = [stack(lanes_of(r_ref, s) * w_inc[s]) for s in every]
    zb = [stack(lanes_of(kka_ref, s) * w_inv[s]).astype(BF16) for s in every]
    zk = [stack(lanes_of(k_ref, s) * w_inv[s]).astype(BF16) for s in every]
    vs = [stack(lanes_of(v_ref, s)).astype(BF16) for s in every]

    lhs = [jnp.concatenate([xa[s], xr[s].astype(BF16)], axis=0) for s in every]
    sb = [_dot_nt(lhs[s], zb[s]) for s in every]
    sk = [_dot_nt(lhs[s], zk[s]) for s in every]
    nmat = [jnp.where(strict, x[0:gc], 0.0) for x in sb]
    mrb = [jnp.where(incl, x[gc:2 * gc], 0.0).astype(BF16) for x in sb]
    lak = [jnp.where(strict, x[0:gc], 0.0).astype(BF16) for x in sk]
    mrk = [jnp.where(incl, x[gc:2 * gc], 0.0).astype(BF16) for x in sk]

    tmat = [eye + x for x in nmat]
    n_b = [x.astype(BF16) for x in nmat]
    npow = [_dot(x, x) for x in n_b]
    power = 2
    while power < chunk:
        np_b = [x.astype(BF16) for x in npow]
        if 2 * power < chunk:
            both = [_dot(jnp.concatenate([tmat[s].astype(BF16), np_b[s]], axis=0), np_b[s]) for s in every]
            tmat = [tmat[s] + both[s][0:gc] for s in every]
            npow = [x[gc:2 * gc] for x in both]
        else:
            tmat = [tmat[s] + _dot(tmat[s].astype(BF16), np_b[s]) for s in every]
        power *= 2
    tmat_b = [x.astype(BF16) for x in tmat]

    lv = [_dot(lak[s], vs[s]).astype(BF16) for s in every]
    pq = [_dot(tmat_b[s], jnp.concatenate([xa[s], lv[s]], axis=1)).astype(BF16) for s in every]
    m_pq = [_dot(mrb[s], pq[s]) for s in every]
    m_v = [_dot(mrk[s], vs[s]) for s in every]
    yp = [(xr[s] + m_pq[s][:, 0:gw]).astype(BF16) for s in every]
    yq = [m_pq[s][:, gw:2 * gw] + m_v[s] for s in every]
    gh = [_dot_tn(pq[s], zb[s]) for s in every]
    vk = [_dot_tn(vs[s], zk[s]) for s in every]
    gmat = [((eye_state + gh[s][0:gw]) * w_end[s]).astype(BF16) for s in every]
    hmat = [(gh[s][gw:2 * gw] + vk[s]) * w_end[s] for s in every]

    s_old = [state_ref[s].astype(BF16) for s in every]
    y = [unstack(_dot_nt(yp[s], s_old[s]) + yq[s]) for s in every]
    s_new = [_dot(s_old[s], gmat[s]) + hmat[s] for s in every]
    for s in every:
        bi, gi = slots[s]
        y_ref[bi, :, gi * gw:(gi + 1) * gw] = y[s]
        state_ref[s] = s_new[s]

    @pl.when(c == pl.num_programs(1) - 1)
    def _():
        for slot, (bi, gi) in enumerate(slots):
            s_new = state_ref[slot]
            folded = s_new[:, 0:R_HEAD_DIM]
            for h in range(1, group_heads):
                folded = folded + s_new[:, h * R_HEAD_DIM:(h + 1) * R_HEAD_DIM]
            sout_ref[bi, gi] = folded


def _wkv_scan(r, ld, k, v, kk, kka, s0, chunk, group_heads, batch_block):
    b, t, _ = r.shape
    gw = group_heads * R_HEAD_DIM
    n_groups = R_WIDTH // gw
    s0g = s0.reshape(b, n_groups, gw, R_HEAD_DIM)
    tok = pl.BlockSpec((batch_block, chunk, R_WIDTH), lambda i, c: (i, c, 0))
    st = pl.BlockSpec((batch_block, n_groups, gw, R_HEAD_DIM), lambda i, c: (i, 0, 0, 0))
    y, s_out = pl.pallas_call(
        functools.partial(_scan_kernel, chunk=chunk, group_heads=group_heads, batch_block=batch_block),
        out_shape=[jax.ShapeDtypeStruct((b, t, R_WIDTH), F32),
                   jax.ShapeDtypeStruct((b, n_groups, gw, R_HEAD_DIM), F32)],
        grid=(b // batch_block, t // chunk),
        in_specs=[tok] * 6 + [st],
        out_specs=[tok, st],
        scratch_shapes=[pltpu.VMEM((batch_block * n_groups, gw, gw), F32)],
        compiler_params=_params(("arbitrary", "arbitrary")),
    )(r, ld, k, v, kk, kka, s0g)
    return y, s_out.reshape(b, R_HEADS, R_HEAD_DIM, R_HEAD_DIM)


def _bias_from_distance(n, table_ref, head):
    n = jnp.maximum(n, 0)
    max_exact = NUM_BUCKETS // 2
    large = (jnp.log(jnp.maximum(n, 1).astype(F32) / max_exact)
             / math.log(MAX_DISTANCE / max_exact)
             * (NUM_BUCKETS - max_exact))
    bias = jnp.full(n.shape, table_ref[NUM_BUCKETS - 1, head], F32)
    for b in range(NUM_BUCKETS - 2, max_exact - 1, -1):
        bias = jnp.where(large < (b + 1 - max_exact), table_ref[b, head], bias)
    for b in range(max_exact):
        bias = jnp.where(n == b, table_ref[b, head], bias)
    return bias


def _lambda(lq1_ref, lk1_ref, lq2_ref, lk2_ref, lam_init):
    s1 = jnp.sum(lq1_ref[...] * lk1_ref[...], axis=-1, keepdims=True)
    s2 = jnp.sum(lq2_ref[...] * lk2_ref[...], axis=-1, keepdims=True)
    return jnp.exp(s1) - jnp.exp(s2) + lam_init


def _subln(o, g_ref, lam_init):
    o = o * lax.rsqrt(jnp.mean(o * o, axis=-1, keepdims=True) + SUBLN_EPS) * g_ref[...]
    return o * (1.0 - lam_init)


def _flash_kernel(table_ref, qt_ref, k_ref, vt_ref, lq1_ref, lk1_ref, lq2_ref, lk2_ref, gcol_ref,
                  o_ref, bias_ref, s_ref, p_ref, acc_ref, *, lam_init):
    b = pl.program_id(0)
    h = pl.program_id(1)
    qi = pl.program_id(2)
    blk = FLASH_BLOCK

    @pl.when((b == 0) & (qi == 0))
    def _():
        sub = BIAS_SUB_TILE
        far = jnp.full((sub, sub), table_ref[NUM_BUCKETS - 1, h], F32)
        di = lax.broadcasted_iota(jnp.int32, (sub, sub), 1) - lax.broadcasted_iota(jnp.int32, (sub, sub), 0)
        for tile in range(3):
            for bj in range(blk // sub):
                for bi in range(blk // sub):
                    lo = tile * blk + (bi - bj) * sub - (sub - 1)
                    if lo >= LAST_BUCKET_FROM:
                        val = far
                    elif lo + 2 * (sub - 1) < 0:
                        val = jnp.full((sub, sub), NEG_INF, F32)
                    else:
                        n = di + (tile * blk + (bi - bj) * sub)
                        val = jnp.where(n >= 0, _bias_from_distance(n, table_ref, h), NEG_INF)
                    bias_ref[h, tile, bj * sub:(bj + 1) * sub, bi * sub:(bi + 1) * sub] = val

    qt = qt_ref[0]
    feat = lax.broadcasted_iota(jnp.int32, (A_V_DIM, 1), 0)
    qs = jnp.concatenate([jnp.where(feat < A_QK_DIM, qt, 0), jnp.where(feat >= A_QK_DIM, qt, 0)], axis=1)
    strip = FLASH_STRIP_ROWS
    strips = range(0, blk, strip)
    n_cols = 2 * blk

    def keys_at(j):
        return pl.ds(pl.multiple_of(j * blk, blk), blk)

    def value_product(j):
        return _dot(vt_ref[0, :, keys_at(j)], p_ref[j % 2])

    def softmax_block(j, m_old, l_old, pending_of):
        s_ref[...] = _dot(k_ref[0, keys_at(j), :], qs)
        pending = None if pending_of is None else value_product(pending_of)
        tile = jnp.minimum(qi - j, 2)

        def biased(r):
            bias = bias_ref[h, tile, r:r + strip, :]
            return s_ref[r:r + strip, :] + jnp.concatenate([bias, bias], axis=1)

        run_max = jnp.full((strip, n_cols), NEG_INF, F32)
        for r in strips:
            run_max = jnp.maximum(run_max, biased(r))
        m_new = jnp.maximum(m_old, jnp.max(run_max, axis=0, keepdims=True))
        alpha = jnp.exp(m_old - m_new)
        run_sum = jnp.zeros((strip, n_cols), F32)
        for r in strips:
            p = jnp.exp(biased(r) - m_new)
            run_sum = run_sum + p
            p_ref[j % 2, r:r + strip, :] = p.astype(BF16)
        l_new = l_old * alpha + jnp.sum(run_sum, axis=0, keepdims=True)
        return m_new, l_new, alpha, pending

    m0 = jnp.full((1, n_cols), NEG_INF, F32)
    m, l, _, _ = softmax_block(0, m0, jnp.zeros((1, n_cols), F32), None)
    acc_ref[...] = jnp.zeros(acc_ref.shape, F32)

    def body(j, carry):
        m_new, l_new, alpha, pending = softmax_block(j, carry[0], carry[1], j - 1)
        acc_ref[...] = (acc_ref[...] + pending) * alpha
        return m_new, l_new

    _, l = lax.fori_loop(1, qi + 1, body, (m, l))
    lam = _lambda(lq1_ref, lk1_ref, lq2_ref, lk2_ref, lam_init)
    o = (acc_ref[...] + value_product(qi)) / l
    o = o[:, 0:blk] - lam * o[:, blk:n_cols]
    o = o * lax.rsqrt(jnp.mean(o * o, axis=0, keepdims=True) + SUBLN_EPS) * gcol_ref[...] * (1.0 - lam_init)
    o_ref[0] = o.T


def _flash_attention(qtb, kb, vtb, table, lam_vecs, subln_g, lam_init):
    b, t, _ = kb.shape
    blk = FLASH_BLOCK
    vec = pl.BlockSpec((1, A_QK_DIM), lambda bb, h, i: (0, 0))
    return pl.pallas_call(
        functools.partial(_flash_kernel, lam_init=lam_init),
        out_shape=jax.ShapeDtypeStruct((b, t, A_WIDTH), F32),
        grid=(b, A_HEADS, t // blk),
        in_specs=[
            pl.BlockSpec(memory_space=pltpu.SMEM),
            pl.BlockSpec((1, A_V_DIM, blk), lambda bb, h, i: (bb, h, i)),
            pl.BlockSpec((1, t, A_V_DIM), lambda bb, h, i: (bb, 0, h)),
            pl.BlockSpec((1, A_V_DIM, t), lambda bb, h, i: (bb, h, 0)),
            vec, vec, vec, vec,
            pl.BlockSpec((A_V_DIM, 1), lambda bb, h, i: (0, 0)),
        ],
        out_specs=pl.BlockSpec((1, blk, A_V_DIM), lambda bb, h, i: (bb, i, h)),
        scratch_shapes=[pltpu.VMEM((A_HEADS, 3, blk, blk), F32),
                        pltpu.VMEM((blk, 2 * blk), F32),
                        pltpu.VMEM((2, blk, 2 * blk), BF16),
                        pltpu.VMEM((A_V_DIM, 2 * blk), F32)],
        compiler_params=_params(("arbitrary", "arbitrary", "arbitrary")),
    )(table, qtb, kb, vtb, *lam_vecs, subln_g.reshape(A_V_DIM, 1))


def _decode_kernel(pt_ref, table_ref, q_ref, kn_ref, vn_ref, lq1_ref, lk1_ref, lq2_ref, lk2_ref, g_ref, *rest,
                   n_pages, n_new, lam_init):
    k_pages = rest[:n_pages]
    v_pages = rest[n_pages:2 * n_pages]
    o_ref = rest[2 * n_pages]
    bias_ref, bias_new_ref = rest[2 * n_pages + 1:]
    del pt_ref
    past = n_pages * PAGE_SIZE
    n_maps = 2

    @pl.when(pl.program_id(0) == 0)
    def _():
        for h in range(A_HEADS):
            tok = lax.broadcasted_iota(jnp.int32, (DECODE_ROWS, past), 0)
            pos = lax.broadcasted_iota(jnp.int32, (DECODE_ROWS, past), 1)
            bias_ref[h] = _bias_from_distance(past + tok - pos, table_ref, h)
            tok = lax.broadcasted_iota(jnp.int32, (DECODE_ROWS, PAGE_SIZE), 0)
            pos = lax.broadcasted_iota(jnp.int32, (DECODE_ROWS, PAGE_SIZE), 1)
            bias_new_ref[h] = jnp.where(pos <= tok, _bias_from_distance(tok - pos, table_ref, h), NEG_INF)

    lam = _lambda(lq1_ref, lk1_ref, lq2_ref, lk2_ref, lam_init)
    heads = range(A_HEADS)
    streams = range(A_HEADS * n_maps)
    qf = [q_ref[0, hm] for hm in streams]
    kt_all = [jnp.concatenate([kp[0, hm * A_QK_DIM:(hm + 1) * A_QK_DIM, :] for kp in k_pages],
                              axis=1).astype(BF16) for hm in streams]
    s_past = [_dot(qf[hm].astype(BF16), kt_all[hm]) + bias_ref[hm // n_maps] for hm in streams]
    s_new = [[jnp.sum(qf[hm] * kn_ref[0, hm, t:t + 1, :], axis=-1, keepdims=True)
              + bias_new_ref[hm // n_maps, :, t:t + 1] for t in range(n_new)] for hm in streams]
    mx = [jnp.max(s, axis=-1, keepdims=True) for s in s_past]
    for t in range(n_new):
        mx = [jnp.maximum(mx[hm], s_new[hm][t]) for hm in streams]
    p_past = [jnp.exp(s_past[hm] - mx[hm]) for hm in streams]
    p_new = [[jnp.exp(s - mx[hm]) for s in s_new[hm]] for hm in streams]
    norm = [jnp.sum(p, axis=-1, keepdims=True) for p in p_past]
    for t in range(n_new):
        norm = [norm[hm] + p_new[hm][t] for hm in streams]
    v_all = [jnp.concatenate([vp[0, pl.ds(h, PAGE_SIZE, stride=A_HEADS), :] for vp in v_pages],
                             axis=0).astype(BF16) for h in heads]
    acc = [_dot(jnp.concatenate([p_past[h * n_maps + m].astype(BF16) for m in range(n_maps)], axis=0), v_all[h])
           for h in heads]
    for h in heads:
        outs = []
        for m in range(n_maps):
            o = acc[h][m * DECODE_ROWS:(m + 1) * DECODE_ROWS]
            for t in range(n_new):
                o = o + p_new[h * n_maps + m][t] * vn_ref[0, h, t:t + 1, :]
            outs.append(o / norm[h * n_maps + m])
        o = _subln(outs[0] - lam * outs[1], g_ref, lam_init)
        o_ref[0, :, h * A_V_DIM:(h + 1) * A_V_DIM] = o[0:n_new]


def _decode_attention(q, k_new, v_new, cache_k, cache_v, page_table, table, lam_vecs, subln_g, lam_init):
    b, n_new, _ = q.shape
    n_pages = page_table.shape[1]
    pad = ((0, 0), (0, 0), (0, DECODE_ROWS - n_new), (0, 0))

    def per_head_map(a):
        a = a.reshape(b, n_new, 2 * A_HEADS, A_QK_DIM)
        return jnp.pad(jnp.transpose(a, (0, 2, 1, 3)), pad)

    vn = jnp.pad(jnp.transpose(v_new.reshape(b, n_new, A_HEADS, A_V_DIM), (0, 2, 1, 3)), pad)
    seq = lambda i, pt: (i, 0, 0, 0)
    fixed = lambda i, pt: (0, 0)

    def k_page(j):
        return pl.BlockSpec((1, QK_COLS, PAGE_SIZE), lambda i, pt: (pt[i, j], 0, 0))

    def v_page(j):
        return pl.BlockSpec((1, PAGE_SIZE * A_HEADS, A_V_DIM), lambda i, pt: (pt[i, j], 0, 0))

    vec = pl.BlockSpec((1, A_QK_DIM), fixed)
    grid_spec = pltpu.PrefetchScalarGridSpec(
        num_scalar_prefetch=1,
        grid=(b,),
        in_specs=[
            pl.BlockSpec(memory_space=pltpu.SMEM),
            pl.BlockSpec((1, 2 * A_HEADS, DECODE_ROWS, A_QK_DIM), seq),
            pl.BlockSpec((1, 2 * A_HEADS, DECODE_ROWS, A_QK_DIM), seq),
            pl.BlockSpec((1, A_HEADS, DECODE_ROWS, A_V_DIM), seq),
            vec, vec, vec, vec,
            pl.BlockSpec((1, A_V_DIM), fixed),
        ] + [k_page(j) for j in range(n_pages)] + [v_page(j) for j in range(n_pages)],
        out_specs=pl.BlockSpec((1, n_new, A_WIDTH), lambda i, pt: (i, 0, 0)),
        scratch_shapes=[pltpu.VMEM((A_HEADS, DECODE_ROWS, n_pages * PAGE_SIZE), F32),
                        pltpu.VMEM((A_HEADS, DECODE_ROWS, PAGE_SIZE), F32)],
    )
    return pl.pallas_call(
        functools.partial(_decode_kernel, n_pages=n_pages, n_new=n_new, lam_init=lam_init),
        out_shape=jax.ShapeDtypeStruct((b, n_new, A_WIDTH), F32),
        grid_spec=grid_spec,
        compiler_params=_params(("arbitrary",)),
    )(page_table, table, per_head_map(q), per_head_map(k_new), vn, *lam_vecs, subln_g,
      *([cache_k] * n_pages), *([cache_v] * n_pages))


def _merge_ffn_kernel(y_ref, r_ref, k_ref, v_ref, g_ref, oa_ref, h_ref, gnw_ref, gnb_ref, rk_ref, seg_ref,
                      wo_ref, gain_ref, gin_ref, gout_ref, win_ref, wo2_ref, out_ref, *, tf):
    seg = seg_ref[...]
    y = y_ref[...]
    mu = _seg_sum(y, seg) * (1.0 / R_HEAD_DIM)
    yc = y - mu
    var = _seg_sum(yc * yc, seg) * (1.0 / R_HEAD_DIM)
    yn = yc * lax.rsqrt(var + GN_EPS) * gnw_ref[...] + gnb_ref[...]
    bonus = _seg_sum(r_ref[...] * k_ref[...] * rk_ref[...], seg) * v_ref[...]
    o_rwkv = ((yn + bonus) * g_ref[...]).astype(BF16)
    m = _dot(o_rwkv, wo_ref[0:R_WIDTH, :]) + _dot(oa_ref[...].astype(BF16), wo_ref[R_WIDTH:R_WIDTH + A_WIDTH, :])
    x = h_ref[...] + _rms(m, gain_ref[...])
    out_ref[...] = _ffn_math(x, gin_ref, gout_ref, win_ref, wo2_ref, tf)


def _merge_ffn(y, r, k, v, g, o_attn, h, P, seg_ones, w_out, gain, g_in, g_out, ff_in, ff_out, tm, tf):
    n = y.shape[0]
    row = lambda i: (i, 0)
    fixed = lambda i: (0, 0)
    resident = pl.Buffered(1)
    tok = pl.BlockSpec((tm, R_WIDTH), row)
    vec = pl.BlockSpec((1, R_WIDTH), fixed)
    wide = pl.BlockSpec((1, D_MODEL), fixed)
    return pl.pallas_call(
        functools.partial(_merge_ffn_kernel, tf=tf),
        out_shape=jax.ShapeDtypeStruct((n, D_MODEL), F32),
        grid=(n // tm,),
        in_specs=[tok, tok, tok, tok, tok, tok,
                  pl.BlockSpec((tm, D_MODEL), row),
                  vec, vec, vec,
                  pl.BlockSpec((R_WIDTH, R_WIDTH), fixed, pipeline_mode=resident),
                  pl.BlockSpec((R_WIDTH + A_WIDTH, D_MODEL), fixed, pipeline_mode=resident),
                  wide, wide, wide,
                  pl.BlockSpec((D_MODEL, 2 * D_FF), fixed, pipeline_mode=resident),
                  pl.BlockSpec((D_FF, D_MODEL), fixed, pipeline_mode=resident)],
        out_specs=pl.BlockSpec((tm, D_MODEL), row),
        compiler_params=_params(("arbitrary",)),
    )(y, r, k, v, g, o_attn, h, P['gn_w'], P['gn_b'], P['rk'], seg_ones, w_out, gain, g_in, g_out, ff_in, ff_out)


def _layer(x, shift0, wkv0, attend, W, *, tm, tf, chunk, group_heads, batch_block, transposed):
    b, t, _ = x.shape
    n = b * t
    g = W['gains']
    h = _ffn_block(x.reshape(n, D_MODEL), g[0:1], g[1:2], W['ff1_in'], W['ff1_out'], tm, tf)
    if t >= tm:
        bnd = shift0.reshape(b, 1, R_COLS)
    else:
        bnd = jnp.repeat(shift0, t, axis=0)
    qb, kb, vb, k, v, tail, r, ld, kr, vr, kk, kka, gate = _proj(
        h, g[2:3], W['w_in'], W['w_q_t'], bnd, W['rw'], W['seg_ones'], tm, t, transposed)
    o_attn = attend(qb, kb, vb, k, v).reshape(n, A_WIDTH)

    def seq(a):
        return a.reshape(b, t, R_WIDTH)

    pad = (-t) % chunk
    scan_in = [seq(a) for a in (r, ld, kr, vr, kk, kka)]
    if pad:
        scan_in = [jnp.pad(a, ((0, 0), (0, pad), (0, 0))) for a in scan_in]
    y, wkv = _wkv_scan(*scan_in, wkv0, chunk, group_heads, batch_block)
    y = y[:, :t].reshape(n, R_WIDTH)
    h = _merge_ffn(y, r, kr, vr, gate, o_attn, h, W['rw'], W['seg_ones'], W['w_out'], g[3:4],
                   g[4:5], g[5:6], W['ff2_in'], W['ff2_out'], tm, tf)
    shift = tail[:, -1] if t >= tm else tail.reshape(b, t, R_COLS)[:, -1]
    if transposed:
        k = jnp.transpose(k.reshape(b, A_HEADS, 2, A_QK_DIM, t), (0, 4, 1, 2, 3))
    else:
        k = k.reshape(b, t, A_HEADS, 2, A_QK_DIM)
    return h.reshape(b, t, D_MODEL), k, v.reshape(b, t, A_HEADS, A_V_DIM), wkv, shift


def kernel(x_prompt, x_sample, cache_k, cache_v, page_table, state_wkv, state_shift, norm_gains, ff1_in, ff1_out, ff2_in, ff2_out, w_in, w_out, rel_bias_table, da_lq1, da_lk1, da_lq2, da_lk2, da_subln, rw_mu, rw_w0, rw_w2, rw_a0, rw_a2, rw_g2, rw_kk, rw_ka, rw_rk, rw_gn_w, rw_gn_b):
    depth = norm_gains.shape[0]
    b, t, _ = x_prompt.shape
    bs, ts, _ = x_sample.shape
    n_pool = cache_k.shape[1]
    lane_head = jnp.arange(R_WIDTH) // R_HEAD_DIM
    seg_ones = (lane_head[:, None] == lane_head[None, :]).astype(BF16)
    lora_pad = jnp.zeros((DECAY_LORA, R_WIDTH), BF16)

    hp, hs = x_prompt, x_sample
    outs = [[] for _ in range(8)]
    for l in range(depth):
        lam_init = 0.8 - 0.6 * math.exp(-0.3 * l)
        W = {
            'gains': norm_gains[l],
            'ff1_in': ff1_in[l].astype(BF16), 'ff1_out': ff1_out[l].astype(BF16),
            'ff2_in': ff2_in[l].astype(BF16), 'ff2_out': ff2_out[l].astype(BF16),
            'w_in': w_in[l].astype(BF16), 'w_out': w_out[l].astype(BF16),
            'w_q_t': w_in[l][:, 0:QK_COLS].T.astype(BF16),
            'seg_ones': seg_ones,
            'rw': {
                'mu': rw_mu[l][None], 'w0': rw_w0[l][None], 'a0': rw_a0[l][None],
                'w2': jnp.concatenate([rw_w2[l].astype(BF16), lora_pad], axis=0),
                'a2': jnp.concatenate([lora_pad, rw_a2[l].astype(BF16)], axis=0),
                'g2': rw_g2[l].astype(BF16),
                'kk': rw_kk[l][None], 'ka': rw_ka[l][None], 'rk': rw_rk[l].reshape(1, R_WIDTH),
                'gn_w': rw_gn_w[l][None], 'gn_b': rw_gn_b[l][None],
            },
        }
        lam_vecs = (da_lq1[l][None], da_lk1[l][None], da_lq2[l][None], da_lk2[l][None])
        subln_g = da_subln[l][None]
        pool_k = jnp.transpose(cache_k[l], (0, 2, 3, 4, 1)).reshape(n_pool, QK_COLS, PAGE_SIZE)
        pool_v = cache_v[l].reshape(n_pool, PAGE_SIZE * A_HEADS, A_V_DIM)

        def attend_p(qtb, kb, vtb, k, v):
            del k, v
            return _flash_attention(qtb, kb.reshape(b, t, QK_COLS), vtb, rel_bias_table, lam_vecs, subln_g, lam_init)

        def attend_s(qb, kb, vb, k, v):
            del kb, vb
            three = lambda a: a.reshape(bs, ts, QK_COLS)
            return _decode_attention(three(qb).astype(F32), three(k), three(v), pool_k, pool_v, page_table,
                                     rel_bias_table, lam_vecs, subln_g, lam_init)

        hp, kp, vp, wp, sp = _layer(hp, jnp.zeros((b, R_COLS), F32),
                                    jnp.zeros((b, R_HEADS, R_HEAD_DIM, R_HEAD_DIM), F32), attend_p, W,
                                    tm=512, tf=D_FF // 2, chunk=SCAN_CHUNK, group_heads=SCAN_GROUP_HEADS,
                                    batch_block=4, transposed=True)
        hs, ks, vs, ws, ss = _layer(hs, state_shift[l], state_wkv[l], attend_s, W,
                                    tm=bs * ts, tf=D_FF // 2, chunk=8, group_heads=SCAN_GROUP_HEADS, batch_block=16,
                                    transposed=False)
        for lst, val in zip(outs, (kp, vp, ks, vs, wp, ws, sp, ss)):
            lst.append(val)

    return (hp, hs) + tuple(jnp.stack(o) for o in outs)
```

```python
import functools
import math

import jax
import jax.numpy as jnp
from jax import lax
from jax.experimental import pallas as pl
from jax.experimental.pallas import tpu as pltpu

F32 = jnp.float32
BF16 = jnp.bfloat16

D_MODEL = 1024
PAGE_SIZE = 128
R_HEADS = 8
R_HEAD_DIM = 64
R_WIDTH = R_HEADS * R_HEAD_DIM
DECAY_LORA = 64
AAA_LORA = 64
GATE_LORA = 128
R_COLS = 3 * R_WIDTH + DECAY_LORA + AAA_LORA + GATE_LORA
A_HEADS = 4
A_QK_DIM = 64
A_V_DIM = 2 * A_QK_DIM
A_WIDTH = A_HEADS * A_V_DIM
QK_COLS = A_HEADS * 2 * A_QK_DIM
A_COLS = 2 * QK_COLS + A_WIDTH
PROJ_COLS = A_COLS + R_COLS
D_FF = 2816
NUM_BUCKETS = 32
MAX_DISTANCE = 128
NORM_EPS = 1e-6
SUBLN_EPS = 1e-5
GN_EPS = 64e-5
NEG_INF = -1e30
LOG2_E = math.log2(math.e)

V7X_VMEM_REQUEST_BYTES = 56 * 1024 * 1024
MXU_COLS = 256
FLASH_BLOCK = 512
FLASH_STRIP_ROWS = 32
BIAS_SUB_TILE = 128
LAST_BUCKET_FROM = MAX_DISTANCE
assert FLASH_BLOCK >= MAX_DISTANCE and FLASH_BLOCK % BIAS_SUB_TILE == 0
DECODE_ROWS = 8
CARRY_ROWS = 8
SCAN_CHUNK = 64
SCAN_GROUP_HEADS = 2


def _params(sem, vmem=V7X_VMEM_REQUEST_BYTES):
    return pltpu.CompilerParams(dimension_semantics=sem, vmem_limit_bytes=vmem)


def _rms(x, g):
    return x * lax.rsqrt(jnp.mean(x * x, axis=-1, keepdims=True) + NORM_EPS) * g


def _dot(a, b):
    return jnp.dot(a, b, preferred_element_type=F32)


def _dot_nt(a, b):
    return lax.dot_general(a, b, (((1,), (1,)), ((), ())), preferred_element_type=F32)


def _dot_tn(a, b):
    return lax.dot_general(a, b, (((0,), (0,)), ((), ())), preferred_element_type=F32)


def _split(x):
    hi = x.astype(BF16)
    lo = (x - hi.astype(F32)).astype(BF16)
    return hi, lo


def _seg_sum(x, seg_ones):
    hi, lo = _split(x)
    return _dot(hi, seg_ones) + _dot(lo, seg_ones)


def _ffn_math(x, gin_ref, gout_ref, win_ref, wo_ref, tf):
    xn = _rms(x, gin_ref[...]).astype(BF16)
    acc = None
    for f in range(0, D_FF, tf):
        gate = _dot(xn, win_ref[:, f:f + tf])
        up = _dot(xn, win_ref[:, D_FF + f:D_FF + f + tf])
        act = (gate * jax.nn.sigmoid(gate) * up).astype(BF16)
        part = _dot(act, wo_ref[f:f + tf, :])
        acc = part if acc is None else acc + part
    return x + 0.5 * _rms(acc, gout_ref[...])


def _ffn_kernel(x_ref, gin_ref, gout_ref, win_ref, wo_ref, o_ref, *, tf):
    o_ref[...] = _ffn_math(x_ref[...], gin_ref, gout_ref, win_ref, wo_ref, tf)


def _ffn_block(x, g_in, g_out, w_in, w_out, tm, tf):
    n = x.shape[0]
    fixed = lambda i: (0, 0)
    resident = pl.Buffered(1)
    return pl.pallas_call(
        functools.partial(_ffn_kernel, tf=tf),
        out_shape=jax.ShapeDtypeStruct((n, D_MODEL), F32),
        grid=(n // tm,),
        in_specs=[
            pl.BlockSpec((tm, D_MODEL), lambda i: (i, 0)),
            pl.BlockSpec((1, D_MODEL), fixed),
            pl.BlockSpec((1, D_MODEL), fixed),
            pl.BlockSpec((D_MODEL, 2 * D_FF), fixed, pipeline_mode=resident),
            pl.BlockSpec((D_FF, D_MODEL), fixed, pipeline_mode=resident),
        ],
        out_specs=pl.BlockSpec((tm, D_MODEL), lambda i: (i, 0)),
        compiler_params=_params(("arbitrary",)),
    )(x, g_in, g_out, w_in, w_out)


def _proj_kernel(h_ref, g_ref, w_ref, wt_ref, bnd_ref, mu_ref, w0_ref, w2_ref, a0_ref, a2_ref, g2_ref,
                 kkw_ref, ka_ref, seg_ref,
                 qb_ref, kb_ref, vb_ref, k_ref, v_ref, tail_ref,
                 r_ref, ld_ref, kr_ref, vr_ref, kk_ref, kka_ref, gate_ref, carry_ref, *, tm, seq_len, transposed):
    i = pl.program_id(0)
    hn = _rms(h_ref[...], g_ref[...]).astype(BF16)
    pr = jnp.concatenate([_dot(hn, w_ref[:, c:c + MXU_COLS]) for c in range(A_COLS, PROJ_COLS, MXU_COLS)], axis=1)
    tail_ref[0] = pr[tm - tail_ref.shape[1]:tm, :]
    rows = lax.broadcasted_iota(jnp.int32, (tm, 1), 0)
    prev = pltpu.roll(pr, 1, axis=0)
    if seq_len >= tm:
        tiles_per_seq = seq_len // tm
        first = jnp.where(i % tiles_per_seq == 0, bnd_ref[0], carry_ref[CARRY_ROWS - 1:CARRY_ROWS, :])
        prev = jnp.where(rows == 0, first, prev)
        carry_ref[...] = pr[tm - CARRY_ROWS:tm, :]
    else:
        prev = jnp.where(rows % seq_len == 0, bnd_ref[...], prev)
    xr = pr + (prev - pr) * mu_ref[...]
    r = xr[:, 0:R_WIDTH]
    kr = xr[:, R_WIDTH:2 * R_WIDTH]
    vr = xr[:, 2 * R_WIDTH:3 * R_WIDTH]
    o = 3 * R_WIDTH
    wa_lo = xr[:, o:o + DECAY_LORA + AAA_LORA]
    g_lo = xr[:, o + DECAY_LORA + AAA_LORA:o + DECAY_LORA + AAA_LORA + GATE_LORA]
    lane = lax.broadcasted_iota(jnp.int32, (1, DECAY_LORA + AAA_LORA), 1)
    w_in = jnp.where(lane < DECAY_LORA, jnp.tanh(wa_lo), 0.0).astype(BF16)
    a_in = jnp.where(lane >= DECAY_LORA, wa_lo, 0.0).astype(BF16)
    g_in = jax.nn.sigmoid(g_lo).astype(BF16)
    kk = kr * kkw_ref[...]
    kk_sq_hi, kk_sq_lo = _split(kk * kk)
    r_ref[...] = r
    vr_ref[...] = vr

    q_scale = A_QK_DIM ** -0.5 * (LOG2_E if transposed else 1.0)
    if transposed:
        qb_ref[0] = (_dot_nt(wt_ref[0:QK_COLS, :], hn) * q_scale).astype(BF16)
    for c in range(0, A_COLS, MXU_COLS):
        if transposed and c < QK_COLS:
            continue
        res = _dot(hn, w_ref[:, c:c + MXU_COLS])
        if c < QK_COLS:
            qb_ref[:, c:c + MXU_COLS] = (res * q_scale).astype(BF16)
        elif c < 2 * QK_COLS:
            kb_ref[:, c - QK_COLS:c - QK_COLS + MXU_COLS] = res.astype(BF16)
            if transposed:
                k_ref[0, c - QK_COLS:c - QK_COLS + MXU_COLS, :] = res.T
            else:
                k_ref[:, c - QK_COLS:c - QK_COLS + MXU_COLS] = res
        else:
            for s in range(0, MXU_COLS, A_V_DIM):
                v_ref[:, (c - 2 * QK_COLS + s) // A_V_DIM, :] = res[:, s:s + A_V_DIM]
            if transposed:
                vb_ref[0, c - 2 * QK_COLS:c - 2 * QK_COLS + MXU_COLS, :] = res.T.astype(BF16)
            else:
                vb_ref[:, c - 2 * QK_COLS:c - 2 * QK_COLS + MXU_COLS] = res.astype(BF16)

    log_w = -jax.nn.softplus(-(w0_ref[...] + _dot(w_in, w2_ref[...]))) - 0.5
    ld_ref[...] = -jnp.exp(log_w)
    a = jax.nn.sigmoid(a0_ref[...] + _dot(a_in, a2_ref[...]))
    gate_ref[...] = _dot(g_in, g2_ref[...])
    norm = jnp.sqrt(_dot(kk_sq_hi, seg_ref[...]) + _dot(kk_sq_lo, seg_ref[...]))
    kk = kk / jnp.maximum(norm, 1e-12)
    kr_ref[...] = kr * (1.0 + (a - 1.0) * ka_ref[...])
    kk_ref[...] = kk
    kka_ref[...] = kk * a


def _proj(h, g, w, wt, bnd, P, seg_ones, tm, seq_len, transposed):
    n = h.shape[0]
    row = lambda i: (i, 0)
    fixed = lambda i: (0, 0)
    rm_shape, rm_spec = (n, QK_COLS), pl.BlockSpec((tm, QK_COLS), row)
    if transposed:
        tiles_per_seq = seq_len // tm
        t_shape = (n // seq_len, QK_COLS, seq_len)
        t_spec = pl.BlockSpec((1, QK_COLS, tm), lambda i: (i // tiles_per_seq, 0, i % tiles_per_seq))
    else:
        t_shape, t_spec = rm_shape, rm_spec
    if seq_len >= tm:
        tiles_per_seq = seq_len // tm
        bnd_spec = pl.BlockSpec((1, 1, R_COLS), lambda i: (i // tiles_per_seq, 0, 0))
        tail_shape = (n // seq_len, CARRY_ROWS, R_COLS)
        tail_spec = pl.BlockSpec((1, CARRY_ROWS, R_COLS), lambda i: (i // tiles_per_seq, 0, 0))
    else:
        bnd_spec = pl.BlockSpec((tm, R_COLS), row)
        tail_shape = (n // tm, tm, R_COLS)
        tail_spec = pl.BlockSpec((1, tm, R_COLS), lambda i: (i, 0, 0))
    heads = jax.ShapeDtypeStruct((n, R_WIDTH), F32)
    vec = pl.BlockSpec((1, R_WIDTH), fixed)
    lora = pl.BlockSpec((DECAY_LORA + AAA_LORA, R_WIDTH), fixed)
    return pl.pallas_call(
        functools.partial(_proj_kernel, tm=tm, seq_len=seq_len, transposed=transposed),
        out_shape=[
            jax.ShapeDtypeStruct(t_shape, BF16),
            jax.ShapeDtypeStruct(rm_shape, BF16),
            jax.ShapeDtypeStruct(t_shape, BF16),
            jax.ShapeDtypeStruct(t_shape, F32),
            jax.ShapeDtypeStruct((n, A_HEADS, A_V_DIM), F32),
            jax.ShapeDtypeStruct(tail_shape, F32),
        ] + [heads] * 7,
        grid=(n // tm,),
        in_specs=[
            pl.BlockSpec((tm, D_MODEL), row),
            pl.BlockSpec((1, D_MODEL), fixed),
            pl.BlockSpec((D_MODEL, PROJ_COLS), fixed),
            pl.BlockSpec((QK_COLS, D_MODEL), fixed),
            bnd_spec,
            pl.BlockSpec((1, R_COLS), fixed),
            vec, lora, vec, lora,
            pl.BlockSpec((GATE_LORA, R_WIDTH), fixed),
            vec, vec,
            pl.BlockSpec((R_WIDTH, R_WIDTH), fixed),
        ],
        out_specs=[t_spec, rm_spec, t_spec, t_spec, pl.BlockSpec((tm, A_HEADS, A_V_DIM), lambda i: (i, 0, 0)),
                   tail_spec] + [pl.BlockSpec((tm, R_WIDTH), row)] * 7,
        scratch_shapes=[pltpu.VMEM((CARRY_ROWS, R_COLS), F32)],
        compiler_params=_params(("arbitrary",)),
    )(h, g, w, wt, bnd, P['mu'], P['w0'], P['w2'], P['a0'], P['a2'], P['g2'], P['kk'], P['ka'], seg_ones)


def _scan_kernel(r_ref, ld_ref, k_ref, v_ref, kk_ref, kka_ref, s0_ref, y_ref, sout_ref, state_ref,
                 *, chunk, group_heads, batch_block):
    c = pl.program_id(1)
    gw = group_heads * R_HEAD_DIM
    gc = group_heads * chunk
    n_groups = R_WIDTH // gw

    srow = lax.broadcasted_iota(jnp.int32, (gc, gw), 0) // chunk
    slane = lax.broadcasted_iota(jnp.int32, (gc, gw), 1) // R_HEAD_DIM
    stack_mask = srow == slane
    mrow = lax.broadcasted_iota(jnp.int32, (gc, gc), 0)
    mcol = lax.broadcasted_iota(jnp.int32, (gc, gc), 1)
    same_head = (mrow // chunk) == (mcol // chunk)
    strict = same_head & (mrow > mcol)
    incl = same_head & (mrow >= mcol)
    eye = (mrow == mcol).astype(F32)
    eye_state = (lax.broadcasted_iota(jnp.int32, (gw, gw), 0)
                 == lax.broadcasted_iota(jnp.int32, (gw, gw), 1)).astype(F32)
    state_rows = lax.broadcasted_iota(jnp.int32, (gw, gw), 0) // R_HEAD_DIM
    state_cols = lax.broadcasted_iota(jnp.int32, (gw, gw), 1) // R_HEAD_DIM
    state_mask = state_rows == state_cols
    trow = lax.broadcasted_iota(jnp.int32, (chunk, gw), 0)

    def stack(x):
        return jnp.where(stack_mask, jnp.concatenate([x] * group_heads, axis=0), 0.0)

    def unstack(x):
        out = x[0:chunk]
        for h in range(1, group_heads):
            out = out + x[h * chunk:(h + 1) * chunk]
        return out

    slots = [(bi, gi) for bi in range(batch_block) for gi in range(n_groups)]

    @pl.when(c == 0)
    def _():
        for slot, (bi, gi) in enumerate(slots):
            spread = jnp.concatenate([s0_ref[bi, gi]] * group_heads, axis=1)
            state_ref[slot] = jnp.where(state_mask, spread, 0.0)

    def lanes_of(ref, s):
        bi, gi = slots[s]
        return ref[bi, :, gi * gw:(gi + 1) * gw]

    every = range(len(slots))
    ld = [lanes_of(ld_ref, s) for s in every]
    cum = list(ld)
    shift = 1
    while shift < chunk:
        cum = [x + jnp.where(trow >= shift, pltpu.roll(x, shift, axis=0), 0.0) for x in cum]
        shift *= 2
    w_inc = [jnp.exp(x) for x in cum]
    w_exc = [jnp.exp(x - d) for x, d in zip(cum, ld)]
    w_inv = [jnp.exp(-x) for x in cum]
    w_end = [x[chunk - 1:chunk, :] for x in w_inc]

    xa = [stack(-lanes_of(kk_ref, s) * w_exc[s]).astype(BF16) for s in every]
    xr = [stack(lanes_of(r_ref, s) * w_inc[s]) for s in every]
    zb = [stack(lanes_of(kka_ref, s) * w_inv[s]).astype(BF16) for s in every]
    zk = [stack(lanes_of(k_ref, s) * w_inv[s]).astype(BF16) for s in every]
    vs = [stack(lanes_of(v_ref, s)).astype(BF16) for s in every]

    lhs = [jnp.concatenate([xa[s], xr[s].astype(BF16)], axis=0) for s in every]
    sb = [_dot_nt(lhs[s], zb[s]) for s in every]
    sk = [_dot_nt(lhs[s], zk[s]) for s in every]
    nmat = [jnp.where(strict, x[0:gc], 0.0) for x in sb]
    mrb = [jnp.where(incl, x[gc:2 * gc], 0.0).astype(BF16) for x in sb]
    lak = [jnp.where(strict, x[0:gc], 0.0).astype(BF16) for x in sk]
    mrk = [jnp.where(incl, x[gc:2 * gc], 0.0).astype(BF16) for x in sk]

    tmat = [eye + x for x in nmat]
    n_b = [x.astype(BF16) for x in nmat]
    npow = [_dot(x, x) for x in n_b]
    power = 2
    while power < chunk:
        np_b = [x.astype(BF16) for x in npow]
        if 2 * power < chunk:
            both = [_dot(jnp.concatenate([tmat[s].astype(BF16), np_b[s]], axis=0), np_b[s]) for s in every]
            tmat = [tmat[s] + both[s][0:gc] for s in every]
            npow = [x[gc:2 * gc] for x in both]
        else:
            tmat = [tmat[s] + _dot(tmat[s].astype(BF16), np_b[s]) for s in every]
        power *= 2
    tmat_b = [x.astype(BF16) for x in tmat]

    lv = [_dot(lak[s], vs[s]).astype(BF16) for s in every]
    pq = [_dot(tmat_b[s], jnp.concatenate([xa[s], lv[s]], axis=1)).astype(BF16) for s in every]
    m_pq = [_dot(mrb[s], pq[s]) for s in every]
    m_v = [_dot(mrk[s], vs[s]) for s in every]
    yp = [(xr[s] + m_pq[s][:, 0:gw]).astype(BF16) for s in every]
    yq = [m_pq[s][:, gw:2 * gw] + m_v[s] for s in every]
    gh = [_dot_tn(pq[s], zb[s]) for s in every]
    vk = [_dot_tn(vs[s], zk[s]) for s in every]
    gmat = [((eye_state + gh[s][0:gw]) * w_end[s]).astype(BF16) for s in every]
    hmat = [(gh[s][gw:2 * gw] + vk[s]) * w_end[s] for s in every]

    s_old = [state_ref[s].astype(BF16) for s in every]
    y = [unstack(_dot_nt(yp[s], s_old[s]) + yq[s]) for s in every]
    s_new = [_dot(s_old[s], gmat[s]) + hmat[s] for s in every]
    for s in every:
        bi, gi = slots[s]
        y_ref[bi, :, gi * gw:(gi + 1) * gw] = y[s]
        state_ref[s] = s_new[s]

    @pl.when(c == pl.num_programs(1) - 1)
    def _():
        for slot, (bi, gi) in enumerate(slots):
            s_new = state_ref[slot]
            folded = s_new[:, 0:R_HEAD_DIM]
            for h in range(1, group_heads):
                folded = folded + s_new[:, h * R_HEAD_DIM:(h + 1) * R_HEAD_DIM]
            sout_ref[bi, gi] = folded


def _wkv_scan(r, ld, k, v, kk, kka, s0, chunk, group_heads, batch_block):
    b, t, _ = r.shape
    gw = group_heads * R_HEAD_DIM
    n_groups = R_WIDTH // gw
    s0g = s0.reshape(b, n_groups, gw, R_HEAD_DIM)
    tok = pl.BlockSpec((batch_block, chunk, R_WIDTH), lambda i, c: (i, c, 0))
    st = pl.BlockSpec((batch_block, n_groups, gw, R_HEAD_DIM), lambda i, c: (i, 0, 0, 0))
    y, s_out = pl.pallas_call(
        functools.partial(_scan_kernel, chunk=chunk, group_heads=group_heads, batch_block=batch_block),
        out_shape=[jax.ShapeDtypeStruct((b, t, R_WIDTH), F32),
                   jax.ShapeDtypeStruct((b, n_groups, gw, R_HEAD_DIM), F32)],
        grid=(b // batch_block, t // chunk),
        in_specs=[tok] * 6 + [st],
        out_specs=[tok, st],
        scratch_shapes=[pltpu.VMEM((batch_block * n_groups, gw, gw), F32)],
        compiler_params=_params(("arbitrary", "arbitrary")),
    )(r, ld, k, v, kk, kka, s0g)
    return y, s_out.reshape(b, R_HEADS, R_HEAD_DIM, R_HEAD_DIM)


def _bias_from_distance(n, table_ref, head):
    n = jnp.maximum(n, 0)
    max_exact = NUM_BUCKETS // 2
    large = (jnp.log(jnp.maximum(n, 1).astype(F32) / max_exact)
             / math.log(MAX_DISTANCE / max_exact)
             * (NUM_BUCKETS - max_exact))
    bias = jnp.full(n.shape, table_ref[NUM_BUCKETS - 1, head], F32)
    for b in range(NUM_BUCKETS - 2, max_exact - 1, -1):
        bias = jnp.where(large < (b + 1 - max_exact), table_ref[b, head], bias)
    for b in range(max_exact):
        bias = jnp.where(n == b, table_ref[b, head], bias)
    return bias


def _lambda(lq1_ref, lk1_ref, lq2_ref, lk2_ref, lam_init):
    s1 = jnp.sum(lq1_ref[...] * lk1_ref[...], axis=-1, keepdims=True)
    s2 = jnp.sum(lq2_ref[...] * lk2_ref[...], axis=-1, keepdims=True)
    return jnp.exp(s1) - jnp.exp(s2) + lam_init


def _subln(o, g_ref, lam_init):
    o = o * lax.rsqrt(jnp.mean(o * o, axis=-1, keepdims=True) + SUBLN_EPS) * g_ref[...]
    return o * (1.0 - lam_init)


def _flash_kernel(table_ref, qt_ref, k_ref, vt_ref, lq1_ref, lk1_ref, lq2_ref, lk2_ref, gcol_ref,
                  o_ref, bias_ref, s_ref, p_ref, acc_ref, *, lam_init):
    b = pl.program_id(0)
    h = pl.program_id(1)
    qi = pl.program_id(2)
    blk = FLASH_BLOCK

    @pl.when((b == 0) & (qi == 0))
    def _():
        sub = BIAS_SUB_TILE
        far = jnp.full((sub, sub), table_ref[NUM_BUCKETS - 1, h], F32)
        di = lax.broadcasted_iota(jnp.int32, (sub, sub), 1) - lax.broadcasted_iota(jnp.int32, (sub, sub), 0)
        for tile in range(3):
            for bj in range(blk // sub):
                for bi in range(blk // sub):
                    lo = tile * blk + (bi - bj) * sub - (sub - 1)
                    if lo >= LAST_BUCKET_FROM:
                        val = far * LOG2_E
                    elif lo + 2 * (sub - 1) < 0:
                        val = jnp.full((sub, sub), NEG_INF, F32)
                    else:
                        n = di + (tile * blk + (bi - bj) * sub)
                        val = jnp.where(n >= 0, _bias_from_distance(n, table_ref, h) * LOG2_E, NEG_INF)
                    bias_ref[h, tile, bj * sub:(bj + 1) * sub, bi * sub:(bi + 1) * sub] = val

    qt = qt_ref[0]
    feat = lax.broadcasted_iota(jnp.int32, (A_V_DIM, 1), 0)
    qs = jnp.concatenate([jnp.where(feat < A_QK_DIM, qt, 0), jnp.where(feat >= A_QK_DIM, qt, 0)], axis=1)
    strip = FLASH_STRIP_ROWS
    strips = range(0, blk, strip)
    n_cols = 2 * blk

    def keys_at(j):
        return pl.ds(pl.multiple_of(j * blk, blk), blk)

    def value_product(j):
        return _dot(vt_ref[0, :, keys_at(j)], p_ref[j % 2])

    def softmax_block(j, m_old, l_old, pending_of):
        s_ref[...] = _dot(k_ref[0, keys_at(j), :], qs)
        pending = None if pending_of is None else value_product(pending_of)
        tile = jnp.minimum(qi - j, 2)

        def biased(r):
            bias = bias_ref[h, tile, r:r + strip, :]
            return s_ref[r:r + strip, :] + jnp.concatenate([bias, bias], axis=1)

        run_max = jnp.full((strip, n_cols), NEG_INF, F32)
        for r in strips:
            t = biased(r)
            s_ref[r:r + strip, :] = t
            run_max = jnp.maximum(run_max, t)
        m_new = jnp.maximum(m_old, jnp.max(run_max, axis=0, keepdims=True))
        alpha = jnp.exp2(m_old - m_new)
        run_sum = jnp.zeros((strip, n_cols), F32)
        for r in strips:
            p = jnp.exp2(s_ref[r:r + strip, :] - m_new)
            run_sum = run_sum + p
            p_ref[j % 2, r:r + strip, :] = p.astype(BF16)
        l_new = l_old * alpha + jnp.sum(run_sum, axis=0, keepdims=True)
        return m_new, l_new, alpha, pending

    m0 = jnp.full((1, n_cols), NEG_INF, F32)
    m, l, _, _ = softmax_block(0, m0, jnp.zeros((1, n_cols), F32), None)
    acc_ref[...] = jnp.zeros(acc_ref.shape, F32)

    def body(j, carry):
        m_new, l_new, alpha, pending = softmax_block(j, carry[0], carry[1], j - 1)
        acc_ref[...] = (acc_ref[...] + pending) * alpha
        return m_new, l_new

    _, l = lax.fori_loop(1, qi + 1, body, (m, l))
    lam = _lambda(lq1_ref, lk1_ref, lq2_ref, lk2_ref, lam_init)
    o = (acc_ref[...] + value_product(qi)) / l
    o = o[:, 0:blk] - lam * o[:, blk:n_cols]
    o = o * lax.rsqrt(jnp.mean(o * o, axis=0, keepdims=True) + SUBLN_EPS) * gcol_ref[...] * (1.0 - lam_init)
    o_ref[0] = o.T


def _flash_attention(qtb, kb, vtb, table, lam_vecs, subln_g, lam_init):
    b, t, _ = kb.shape
    blk = FLASH_BLOCK
    vec = pl.BlockSpec((1, A_QK_DIM), lambda bb, h, i: (0, 0))
    return pl.pallas_call(
        functools.partial(_flash_kernel, lam_init=lam_init),
        out_shape=jax.ShapeDtypeStruct((b, t, A_WIDTH), F32),
        grid=(b, A_HEADS, t // blk),
        in_specs=[
            pl.BlockSpec(memory_space=pltpu.SMEM),
            pl.BlockSpec((1, A_V_DIM, blk), lambda bb, h, i: (bb, h, i)),
            pl.BlockSpec((1, t, A_V_DIM), lambda bb, h, i: (bb, 0, h)),
            pl.BlockSpec((1, A_V_DIM, t), lambda bb, h, i: (bb, h, 0)),
            vec, vec, vec, vec,
            pl.BlockSpec((A_V_DIM, 1), lambda bb, h, i: (0, 0)),
        ],
        out_specs=pl.BlockSpec((1, blk, A_V_DIM), lambda bb, h, i: (bb, i, h)),
        scratch_shapes=[pltpu.VMEM((A_HEADS, 3, blk, blk), F32),
                        pltpu.VMEM((blk, 2 * blk), F32),
                        pltpu.VMEM((2, blk, 2 * blk), BF16),
                        pltpu.VMEM((A_V_DIM, 2 * blk), F32)],
        compiler_params=_params(("arbitrary", "arbitrary", "arbitrary")),
    )(table, qtb, kb, vtb, *lam_vecs, subln_g.reshape(A_V_DIM, 1))


def _decode_kernel(pt_ref, table_ref, q_ref, kn_ref, vn_ref, lq1_ref, lk1_ref, lq2_ref, lk2_ref, g_ref, *rest,
                   n_pages, n_new, lam_init):
    k_pages = rest[:n_pages]
    v_pages = rest[n_pages:2 * n_pages]
    o_ref = rest[2 * n_pages]
    bias_ref, bias_new_ref = rest[2 * n_pages + 1:]
    del pt_ref
    past = n_pages * PAGE_SIZE
    n_maps = 2

    @pl.when(pl.program_id(0) == 0)
    def _():
        for h in range(A_HEADS):
            tok = lax.broadcasted_iota(jnp.int32, (DECODE_ROWS, past), 0)
            pos = lax.broadcasted_iota(jnp.int32, (DECODE_ROWS, past), 1)
            bias_ref[h] = _bias_from_distance(past + tok - pos, table_ref, h)
            tok = lax.broadcasted_iota(jnp.int32, (DECODE_ROWS, PAGE_SIZE), 0)
            pos = lax.broadcasted_iota(jnp.int32, (DECODE_ROWS, PAGE_SIZE), 1)
            bias_new_ref[h] = jnp.where(pos <= tok, _bias_from_distance(tok - pos, table_ref, h), NEG_INF)

    lam = _lambda(lq1_ref, lk1_ref, lq2_ref, lk2_ref, lam_init)
    heads = range(A_HEADS)
    streams = range(A_HEADS * n_maps)
    qf = [q_ref[0, hm] for hm in streams]
    kt_all = [jnp.concatenate([kp[0, hm * A_QK_DIM:(hm + 1) * A_QK_DIM, :] for kp in k_pages],
                              axis=1).astype(BF16) for hm in streams]
    s_past = [_dot(qf[hm].astype(BF16), kt_all[hm]) + bias_ref[hm // n_maps] for hm in streams]
    s_new = [[jnp.sum(qf[hm] * kn_ref[0, hm, t:t + 1, :], axis=-1, keepdims=True)
              + bias_new_ref[hm // n_maps, :, t:t + 1] for t in range(n_new)] for hm in streams]
    mx = [jnp.max(s, axis=-1, keepdims=True) for s in s_past]
    for t in range(n_new):
        mx = [jnp.maximum(mx[hm], s_new[hm][t]) for hm in streams]
    p_past = [jnp.exp(s_past[hm] - mx[hm]) for hm in streams]
    p_new = [[jnp.exp(s - mx[hm]) for s in s_new[hm]] for hm in streams]
    norm = [jnp.sum(p, axis=-1, keepdims=True) for p in p_past]
    for t in range(n_new):
        norm = [norm[hm] + p_new[hm][t] for hm in streams]
    v_all = [jnp.concatenate([vp[0, pl.ds(h, PAGE_SIZE, stride=A_HEADS), :] for vp in v_pages],
                             axis=0).astype(BF16) for h in heads]
    acc = [_dot(jnp.concatenate([p_past[h * n_maps + m].astype(BF16) for m in range(n_maps)], axis=0), v_all[h])
           for h in heads]
    for h in heads:
        outs = []
        for m in range(n_maps):
            o = acc[h][m * DECODE_ROWS:(m + 1) * DECODE_ROWS]
            for t in range(n_new):
                o = o + p_new[h * n_maps + m][t] * vn_ref[0, h, t:t + 1, :]
            outs.append(o / norm[h * n_maps + m])
        o = _subln(outs[0] - lam * outs[1], g_ref, lam_init)
        o_ref[0, :, h * A_V_DIM:(h + 1) * A_V_DIM] = o[0:n_new]


def _decode_attention(q, k_new, v_new, cache_k, cache_v, page_table, table, lam_vecs, subln_g, lam_init):
    b, n_new, _ = q.shape
    n_pages = page_table.shape[1]
    pad = ((0, 0), (0, 0), (0, DECODE_ROWS - n_new), (0, 0))

    def per_head_map(a):
        a = a.reshape(b, n_new, 2 * A_HEADS, A_QK_DIM)
        return jnp.pad(jnp.transpose(a, (0, 2, 1, 3)), pad)

    vn = jnp.pad(jnp.transpose(v_new.reshape(b, n_new, A_HEADS, A_V_DIM), (0, 2, 1, 3)), pad)
    seq = lambda i, pt: (i, 0, 0, 0)
    fixed = lambda i, pt: (0, 0)

    def k_page(j):
        return pl.BlockSpec((1, QK_COLS, PAGE_SIZE), lambda i, pt: (pt[i, j], 0, 0))

    def v_page(j):
        return pl.BlockSpec((1, PAGE_SIZE * A_HEADS, A_V_DIM), lambda i, pt: (pt[i, j], 0, 0))

    vec = pl.BlockSpec((1, A_QK_DIM), fixed)
    grid_spec = pltpu.PrefetchScalarGridSpec(
        num_scalar_prefetch=1,
        grid=(b,),
        in_specs=[
            pl.BlockSpec(memory_space=pltpu.SMEM),
            pl.BlockSpec((1, 2 * A_HEADS, DECODE_ROWS, A_QK_DIM), seq),
            pl.BlockSpec((1, 2 * A_HEADS, DECODE_ROWS, A_QK_DIM), seq),
            pl.BlockSpec((1, A_HEADS, DECODE_ROWS, A_V_DIM), seq),
            vec, vec, vec, vec,
            pl.BlockSpec((1, A_V_DIM), fixed),
        ] + [k_page(j) for j in range(n_pages)] + [v_page(j) for j in range(n_pages)],
        out_specs=pl.BlockSpec((1, n_new, A_WIDTH), lambda i, pt: (i, 0, 0)),
        scratch_shapes=[pltpu.VMEM((A_HEADS, DECODE_ROWS, n_pages * PAGE_SIZE), F32),
                        pltpu.VMEM((A_HEADS, DECODE_ROWS, PAGE_SIZE), F32)],
    )
    return pl.pallas_call(
        functools.partial(_decode_kernel, n_pages=n_pages, n_new=n_new, lam_init=lam_init),
        out_shape=jax.ShapeDtypeStruct((b, n_new, A_WIDTH), F32),
        grid_spec=grid_spec,
        compiler_params=_params(("arbitrary",)),
    )(page_table, table, per_head_map(q), per_head_map(k_new), vn, *lam_vecs, subln_g,
      *([cache_k] * n_pages), *([cache_v] * n_pages))


def _merge_ffn_kernel(y_ref, r_ref, k_ref, v_ref, g_ref, oa_ref, h_ref, gnw_ref, gnb_ref, rk_ref, seg_ref,
                      wo_ref, gain_ref, gin_ref, gout_ref, win_ref, wo2_ref, out_ref, *, tf):
    seg = seg_ref[...]
    y = y_ref[...]
    mu = _seg_sum(y, seg) * (1.0 / R_HEAD_DIM)
    yc = y - mu
    var = _seg_sum(yc * yc, seg) * (1.0 / R_HEAD_DIM)
    yn = yc * lax.rsqrt(var + GN_EPS) * gnw_ref[...] + gnb_ref[...]
    bonus = _seg_sum(r_ref[...] * k_ref[...] * rk_ref[...], seg) * v_ref[...]
    o_rwkv = ((yn + bonus) * g_ref[...]).astype(BF16)
    m = _dot(o_rwkv, wo_ref[0:R_WIDTH, :]) + _dot(oa_ref[...].astype(BF16), wo_ref[R_WIDTH:R_WIDTH + A_WIDTH, :])
    x = h_ref[...] + _rms(m, gain_ref[...])
    out_ref[...] = _ffn_math(x, gin_ref, gout_ref, win_ref, wo2_ref, tf)


def _merge_ffn(y, r, k, v, g, o_attn, h, P, seg_ones, w_out, gain, g_in, g_out, ff_in, ff_out, tm, tf):
    n = y.shape[0]
    row = lambda i: (i, 0)
    fixed = lambda i: (0, 0)
    resident = pl.Buffered(1)
    tok = pl.BlockSpec((tm, R_WIDTH), row)
    vec = pl.BlockSpec((1, R_WIDTH), fixed)
    wide = pl.BlockSpec((1, D_MODEL), fixed)
    return pl.pallas_call(
        functools.partial(_merge_ffn_kernel, tf=tf),
        out_shape=jax.ShapeDtypeStruct((n, D_MODEL), F32),
        grid=(n // tm,),
        in_specs=[tok, tok, tok, tok, tok, tok,
                  pl.BlockSpec((tm, D_MODEL), row),
                  vec, vec, vec,
                  pl.BlockSpec((R_WIDTH, R_WIDTH), fixed, pipeline_mode=resident),
                  pl.BlockSpec((R_WIDTH + A_WIDTH, D_MODEL), fixed, pipeline_mode=resident),
                  wide, wide, wide,
                  pl.BlockSpec((D_MODEL, 2 * D_FF), fixed, pipeline_mode=resident),
                  pl.BlockSpec((D_FF, D_MODEL), fixed, pipeline_mode=resident)],
        out_specs=pl.BlockSpec((tm, D_MODEL), row),
        compiler_params=_params(("arbitrary",)),
    )(y, r, k, v, g, o_attn, h, P['gn_w'], P['gn_b'], P['rk'], seg_ones, w_out, gain, g_in, g_out, ff_in, ff_out)


def _layer(x, shift0, wkv0, attend, W, *, tm, tf, chunk, group_heads, batch_block, transposed):
    b, t, _ = x.shape
    n = b * t
    g = W['gains']
    h = _ffn_block(x.reshape(n, D_MODEL), g[0:1], g[1:2], W['ff1_in'], W['ff1_out'], tm, tf)
    if t >= tm:
        bnd = shift0.reshape(b, 1, R_COLS)
    else:
        bnd = jnp.repeat(shift0, t, axis=0)
    qb, kb, vb, k, v, tail, r, ld, kr, vr, kk, kka, gate = _proj(
        h, g[2:3], W['w_in'], W['w_q_t'], bnd, W['rw'], W['seg_ones'], tm, t, transposed)
    o_attn = attend(qb, kb, vb, k, v).reshape(n, A_WIDTH)

    def seq(a):
        return a.reshape(b, t, R_WIDTH)

    pad = (-t) % chunk
    scan_in = [seq(a) for a in (r, ld, kr, vr, kk, kka)]
    if pad:
        scan_in = [jnp.pad(a, ((0, 0), (0, pad), (0, 0))) for a in scan_in]
    y, wkv = _wkv_scan(*scan_in, wkv0, chunk, group_heads, batch_block)
    y = y[:, :t].reshape(n, R_WIDTH)
    h = _merge_ffn(y, r, kr, vr, gate, o_attn, h, W['rw'], W['seg_ones'], W['w_out'], g[3:4],
                   g[4:5], g[5:6], W['ff2_in'], W['ff2_out'], tm, tf)
    shift = tail[:, -1] if t >= tm else tail.reshape(b, t, R_COLS)[:, -1]
    if transposed:
        k = jnp.transpose(k.reshape(b, A_HEADS, 2, A_QK_DIM, t), (0, 4, 1, 2, 3))
    else:
        k = k.reshape(b, t, A_HEADS, 2, A_QK_DIM)
    return h.reshape(b, t, D_MODEL), k, v.reshape(b, t, A_HEADS, A_V_DIM), wkv, shift


def kernel(x_prompt, x_sample, cache_k, cache_v, page_table, state_wkv, state_shift, norm_gains, ff1_in, ff1_out, ff2_in, ff2_out, w_in, w_out, rel_bias_table, da_lq1, da_lk1, da_lq2, da_lk2, da_subln, rw_mu, rw_w0, rw_w2, rw_a0, rw_a2, rw_g2, rw_kk, rw_ka, rw_rk, rw_gn_w, rw_gn_b):
    depth = norm_gains.shape[0]
    b, t, _ = x_prompt.shape
    bs, ts, _ = x_sample.shape
    n_pool = cache_k.shape[1]
    lane_head = jnp.arange(R_WIDTH) // R_HEAD_DIM
    seg_ones = (lane_head[:, None] == lane_head[None, :]).astype(BF16)
    lora_pad = jnp.zeros((DECAY_LORA, R_WIDTH), BF16)

    hp, hs = x_prompt, x_sample
    outs = [[] for _ in range(8)]
    for l in range(depth):
        lam_init = 0.8 - 0.6 * math.exp(-0.3 * l)
        W = {
            'gains': norm_gains[l],
            'ff1_in': ff1_in[l].astype(BF16), 'ff1_out': ff1_out[l].astype(BF16),
            'ff2_in': ff2_in[l].astype(BF16), 'ff2_out': ff2_out[l].astype(BF16),
            'w_in': w_in[l].astype(BF16), 'w_out': w_out[l].astype(BF16),
            'w_q_t': w_in[l][:, 0:QK_COLS].T.astype(BF16),
            'seg_ones': seg_ones,
            'rw': {
                'mu': rw_mu[l][None], 'w0': rw_w0[l][None], 'a0': rw_a0[l][None],
                'w2': jnp.concatenate([rw_w2[l].astype(BF16), lora_pad], axis=0),
                'a2': jnp.concatenate([lora_pad, rw_a2[l].astype(BF16)], axis=0),
                'g2': rw_g2[l].astype(BF16),
                'kk': rw_kk[l][None], 'ka': rw_ka[l][None], 'rk': rw_rk[l].reshape(1, R_WIDTH),
                'gn_w': rw_gn_w[l][None], 'gn_b': rw_gn_b[l][None],
            },
        }
        lam_vecs = (da_lq1[l][None], da_lk1[l][None], da_lq2[l][None], da_lk2[l][None])
        subln_g = da_subln[l][None]
        pool_k = jnp.transpose(cache_k[l], (0, 2, 3, 4, 1)).reshape(n_pool, QK_COLS, PAGE_SIZE)
        pool_v = cache_v[l].reshape(n_pool, PAGE_SIZE * A_HEADS, A_V_DIM)

        def attend_p(qtb, kb, vtb, k, v):
            del k, v
            return _flash_attention(qtb, kb.reshape(b, t, QK_COLS), vtb, rel_bias_table, lam_vecs, subln_g, lam_init)

        def attend_s(qb, kb, vb, k, v):
            del kb, vb
            three = lambda a: a.reshape(bs, ts, QK_COLS)
            return _decode_attention(three(qb).astype(F32), three(k), three(v), pool_k, pool_v, page_table,
                                     rel_bias_table, lam_vecs, subln_g, lam_init)

        hp, kp, vp, wp, sp = _layer(hp, jnp.zeros((b, R_COLS), F32),
                                    jnp.zeros((b, R_HEADS, R_HEAD_DIM, R_HEAD_DIM), F32), attend_p, W,
                                    tm=512, tf=D_FF // 2, chunk=SCAN_CHUNK, group_heads=SCAN_GROUP_HEADS,
                                    batch_block=4, transposed=True)
        hs, ks, vs, ws, ss = _layer(hs, state_shift[l], state_wkv[l], attend_s, W,
                                    tm=bs * ts, tf=D_FF // 2, chunk=8, group_heads=SCAN_GROUP_HEADS, batch_block=16,
                                    transposed=False)
        for lst, val in zip(outs, (kp, vp, ks, vs, wp, ws, sp, ss)):
            lst.append(val)

    return (hp, hs) + tuple(jnp.stack(o) for o in outs)
```

```python
import functools
import math

import jax
import jax.numpy as jnp
from jax import lax
from jax.experimental import pallas as pl
from jax.experimental.pallas import tpu as pltpu

F32 = jnp.float32
BF16 = jnp.bfloat16

D_MODEL = 1024
PAGE_SIZE = 128
R_HEADS = 8
R_HEAD_DIM = 64
R_WIDTH = R_HEADS * R_HEAD_DIM
DECAY_LORA = 64
AAA_LORA = 64
GATE_LORA = 128
R_COLS = 3 * R_WIDTH + DECAY_LORA + AAA_LORA + GATE_LORA
A_HEADS = 4
A_QK_DIM = 64
A_V_DIM = 2 * A_QK_DIM
A_WIDTH = A_HEADS * A_V_DIM
QK_COLS = A_HEADS * 2 * A_QK_DIM
A_COLS = 2 * QK_COLS + A_WIDTH
PROJ_COLS = A_COLS + R_COLS
D_FF = 2816
NUM_BUCKETS = 32
MAX_DISTANCE = 128
NORM_EPS = 1e-6
SUBLN_EPS = 1e-5
GN_EPS = 64e-5
NEG_INF = -1e30
LOG2_E = math.log2(math.e)

V7X_VMEM_REQUEST_BYTES = 56 * 1024 * 1024
MXU_COLS = 256
FLASH_BLOCK = 512
FLASH_STRIP_ROWS = 32
BIAS_SUB_TILE = 128
LAST_BUCKET_FROM = MAX_DISTANCE
assert FLASH_BLOCK >= MAX_DISTANCE and FLASH_BLOCK % BIAS_SUB_TILE == 0
DECODE_ROWS = 8
CARRY_ROWS = 8
SCAN_CHUNK = 64
SCAN_GROUP_HEADS = 2


def _params(sem, vmem=V7X_VMEM_REQUEST_BYTES):
    return pltpu.CompilerParams(dimension_semantics=sem, vmem_limit_bytes=vmem)


def _rms(x, g):
    return x * lax.rsqrt(jnp.mean(x * x, axis=-1, keepdims=True) + NORM_EPS) * g


def _dot(a, b):
    return jnp.dot(a, b, preferred_element_type=F32)


def _dot_nt(a, b):
    return lax.dot_general(a, b, (((1,), (1,)), ((), ())), preferred_element_type=F32)


def _dot_tn(a, b):
    return lax.dot_general(a, b, (((0,), (0,)), ((), ())), preferred_element_type=F32)


def _split(x):
    hi = x.astype(BF16)
    lo = (x - hi.astype(F32)).astype(BF16)
    return hi, lo


def _seg_sum(x, seg_ones):
    hi, lo = _split(x)
    return _dot(hi, seg_ones) + _dot(lo, seg_ones)


def _ffn_math(x, gin_ref, gout_ref, win_ref, wo_ref, tf):
    xn = _rms(x, gin_ref[...]).astype(BF16)
    acc = None
    for f in range(0, D_FF, tf):
        gate = _dot(xn, win_ref[:, f:f + tf])
        up = _dot(xn, win_ref[:, D_FF + f:D_FF + f + tf])
        act = (gate * jax.nn.sigmoid(gate) * up).astype(BF16)
        part = _dot(act, wo_ref[f:f + tf, :])
        acc = part if acc is None else acc + part
    return x + 0.5 * _rms(acc, gout_ref[...])


def _ffn_kernel(x_ref, gin_ref, gout_ref, win_ref, wo_ref, o_ref, *, tf):
    o_ref[...] = _ffn_math(x_ref[...], gin_ref, gout_ref, win_ref, wo_ref, tf)


def _ffn_block(x, g_in, g_out, w_in, w_out, tm, tf):
    n = x.shape[0]
    fixed = lambda i: (0, 0)
    resident = pl.Buffered(1)
    return pl.pallas_call(
        functools.partial(_ffn_kernel, tf=tf),
        out_shape=jax.ShapeDtypeStruct((n, D_MODEL), F32),
        grid=(n // tm,),
        in_specs=[
            pl.BlockSpec((tm, D_MODEL), lambda i: (i, 0)),
            pl.BlockSpec((1, D_MODEL), fixed),
            pl.BlockSpec((1, D_MODEL), fixed),
            pl.BlockSpec((D_MODEL, 2 * D_FF), fixed, pipeline_mode=resident),
            pl.BlockSpec((D_FF, D_MODEL), fixed, pipeline_mode=resident),
        ],
        out_specs=pl.BlockSpec((tm, D_MODEL), lambda i: (i, 0)),
        compiler_params=_params(("arbitrary",)),
    )(x, g_in, g_out, w_in, w_out)


def _proj_kernel(h_ref, g_ref, w_ref, wt_ref, bnd_ref, mu_ref, w0_ref, w2_ref, a0_ref, a2_ref, g2_ref,
                 kkw_ref, ka_ref, seg_ref,
                 qb_ref, kb_ref, vb_ref, k_ref, v_ref, tail_ref,
                 r_ref, ld_ref, kr_ref, vr_ref, kk_ref, kka_ref, gate_ref, carry_ref, *, tm, seq_len, transposed):
    i = pl.program_id(0)
    hn = _rms(h_ref[...], g_ref[...]).astype(BF16)
    pr = jnp.concatenate([_dot(hn, w_ref[:, c:c + MXU_COLS]) for c in range(A_COLS, PROJ_COLS, MXU_COLS)], axis=1)
    tail_ref[0] = pr[tm - tail_ref.shape[1]:tm, :]
    rows = lax.broadcasted_iota(jnp.int32, (tm, 1), 0)
    prev = pltpu.roll(pr, 1, axis=0)
    if seq_len >= tm:
        tiles_per_seq = seq_len // tm
        first = jnp.where(i % tiles_per_seq == 0, bnd_ref[0], carry_ref[CARRY_ROWS - 1:CARRY_ROWS, :])
        prev = jnp.where(rows == 0, first, prev)
        carry_ref[...] = pr[tm - CARRY_ROWS:tm, :]
    else:
        prev = jnp.where(rows % seq_len == 0, bnd_ref[...], prev)
    xr = pr + (prev - pr) * mu_ref[...]
    r = xr[:, 0:R_WIDTH]
    kr = xr[:, R_WIDTH:2 * R_WIDTH]
    vr = xr[:, 2 * R_WIDTH:3 * R_WIDTH]
    o = 3 * R_WIDTH
    wa_lo = xr[:, o:o + DECAY_LORA + AAA_LORA]
    g_lo = xr[:, o + DECAY_LORA + AAA_LORA:o + DECAY_LORA + AAA_LORA + GATE_LORA]
    lane = lax.broadcasted_iota(jnp.int32, (1, DECAY_LORA + AAA_LORA), 1)
    w_in = jnp.where(lane < DECAY_LORA, jnp.tanh(wa_lo), 0.0).astype(BF16)
    a_in = jnp.where(lane >= DECAY_LORA, wa_lo, 0.0).astype(BF16)
    g_in = jax.nn.sigmoid(g_lo).astype(BF16)
    kk = kr * kkw_ref[...]
    kk_sq_hi, kk_sq_lo = _split(kk * kk)
    r_ref[...] = r
    vr_ref[...] = vr

    q_scale = A_QK_DIM ** -0.5 * (LOG2_E if transposed else 1.0)
    if transposed:
        qb_ref[0] = (_dot_nt(wt_ref[0:QK_COLS, :], hn) * q_scale).astype(BF16)
    for c in range(0, A_COLS, MXU_COLS):
        if transposed and c < QK_COLS:
            continue
        res = _dot(hn, w_ref[:, c:c + MXU_COLS])
        if c < QK_COLS:
            qb_ref[:, c:c + MXU_COLS] = (res * q_scale).astype(BF16)
        elif c < 2 * QK_COLS:
            kb_ref[:, c - QK_COLS:c - QK_COLS + MXU_COLS] = res.astype(BF16)
            if transposed:
                k_ref[0, c - QK_COLS:c - QK_COLS + MXU_COLS, :] = res.T
            else:
                k_ref[:, c - QK_COLS:c - QK_COLS + MXU_COLS] = res
        else:
            for s in range(0, MXU_COLS, A_V_DIM):
                v_ref[:, (c - 2 * QK_COLS + s) // A_V_DIM, :] = res[:, s:s + A_V_DIM]
            if transposed:
                vb_ref[0, c - 2 * QK_COLS:c - 2 * QK_COLS + MXU_COLS, :] = res.T.astype(BF16)
            else:
                vb_ref[:, c - 2 * QK_COLS:c - 2 * QK_COLS + MXU_COLS] = res.astype(BF16)

    log_w = -jax.nn.softplus(-(w0_ref[...] + _dot(w_in, w2_ref[...]))) - 0.5
    ld_ref[...] = -jnp.exp(log_w)
    a = jax.nn.sigmoid(a0_ref[...] + _dot(a_in, a2_ref[...]))
    gate_ref[...] = _dot(g_in, g2_ref[...])
    norm = jnp.sqrt(_dot(kk_sq_hi, seg_ref[...]) + _dot(kk_sq_lo, seg_ref[...]))
    kk = kk / jnp.maximum(norm, 1e-12)
    kr_ref[...] = kr * (1.0 + (a - 1.0) * ka_ref[...])
    kk_ref[...] = kk
    kka_ref[...] = kk * a


def _proj(h, g, w, wt, bnd, P, seg_ones, tm, seq_len, transposed):
    n = h.shape[0]
    row = lambda i: (i, 0)
    fixed = lambda i: (0, 0)
    rm_shape, rm_spec = (n, QK_COLS), pl.BlockSpec((tm, QK_COLS), row)
    if transposed:
        tiles_per_seq = seq_len // tm
        t_shape = (n // seq_len, QK_COLS, seq_len)
        t_spec = pl.BlockSpec((1, QK_COLS, tm), lambda i: (i // tiles_per_seq, 0, i % tiles_per_seq))
    else:
        t_shape, t_spec = rm_shape, rm_spec
    if seq_len >= tm:
        tiles_per_seq = seq_len // tm
        bnd_spec = pl.BlockSpec((1, 1, R_COLS), lambda i: (i // tiles_per_seq, 0, 0))
        tail_shape = (n // seq_len, CARRY_ROWS, R_COLS)
        tail_spec = pl.BlockSpec((1, CARRY_ROWS, R_COLS), lambda i: (i // tiles_per_seq, 0, 0))
    else:
        bnd_spec = pl.BlockSpec((tm, R_COLS), row)
        tail_shape = (n // tm, tm, R_COLS)
        tail_spec = pl.BlockSpec((1, tm, R_COLS), lambda i: (i, 0, 0))
    heads = jax.ShapeDtypeStruct((n, R_WIDTH), F32)
    vec = pl.BlockSpec((1, R_WIDTH), fixed)
    lora = pl.BlockSpec((DECAY_LORA + AAA_LORA, R_WIDTH), fixed)
    return pl.pallas_call(
        functools.partial(_proj_kernel, tm=tm, seq_len=seq_len, transposed=transposed),
        out_shape=[
            jax.ShapeDtypeStruct(t_shape, BF16),
            jax.ShapeDtypeStruct(rm_shape, BF16),
            jax.ShapeDtypeStruct(t_shape, BF16),
            jax.ShapeDtypeStruct(t_shape, F32),
            jax.ShapeDtypeStruct((n, A_HEADS, A_V_DIM), F32),
            jax.ShapeDtypeStruct(tail_shape, F32),
        ] + [heads] * 7,
        grid=(n // tm,),
        in_specs=[
            pl.BlockSpec((tm, D_MODEL), row),
            pl.BlockSpec((1, D_MODEL), fixed),
            pl.BlockSpec((D_MODEL, PROJ_COLS), fixed),
            pl.BlockSpec((QK_COLS, D_MODEL), fixed),
            bnd_spec,
            pl.BlockSpec((1, R_COLS), fixed),
            vec, lora, vec, lora,
            pl.BlockSpec((GATE_LORA, R_WIDTH), fixed),
            vec, vec,
            pl.BlockSpec((R_WIDTH, R_WIDTH), fixed),
        ],
        out_specs=[t_spec, rm_spec, t_spec, t_spec, pl.BlockSpec((tm, A_HEADS, A_V_DIM), lambda i: (i, 0, 0)),
                   tail_spec] + [pl.BlockSpec((tm, R_WIDTH), row)] * 7,
        scratch_shapes=[pltpu.VMEM((CARRY_ROWS, R_COLS), F32)],
        compiler_params=_params(("arbitrary",)),
    )(h, g, w, wt, bnd, P['mu'], P['w0'], P['w2'], P['a0'], P['a2'], P['g2'], P['kk'], P['ka'], seg_ones)


def _scan_kernel(r_ref, ld_ref, k_ref, v_ref, kk_ref, kka_ref, s0_ref, y_ref, sout_ref, state_ref,
                 *, chunk, group_heads, batch_block):
    c = pl.program_id(1)
    gw = group_heads * R_HEAD_DIM
    gc = group_heads * chunk
    n_groups = R_WIDTH // gw

    srow = lax.broadcasted_iota(jnp.int32, (gc, gw), 0) // chunk
    slane = lax.broadcasted_iota(jnp.int32, (gc, gw), 1) // R_HEAD_DIM
    stack_mask = srow == slane
    mrow = lax.broadcasted_iota(jnp.int32, (gc, gc), 0)
    mcol = lax.broadcasted_iota(jnp.int32, (gc, gc), 1)
    same_head = (mrow // chunk) == (mcol // chunk)
    strict = same_head & (mrow > mcol)
    incl = same_head & (mrow >= mcol)
    eye = (mrow == mcol).astype(F32)
    eye_state = (lax.broadcasted_iota(jnp.int32, (gw, gw), 0)
                 == lax.broadcasted_iota(jnp.int32, (gw, gw), 1)).astype(F32)
    state_rows = lax.broadcasted_iota(jnp.int32, (gw, gw), 0) // R_HEAD_DIM
    state_cols = lax.broadcasted_iota(jnp.int32, (gw, gw), 1) // R_HEAD_DIM
    state_mask = state_rows == state_cols
    trow = lax.broadcasted_iota(jnp.int32, (chunk, gw), 0)

    def stack(x):
        return jnp.where(stack_mask, jnp.concatenate([x] * group_heads, axis=0), 0.0)

    def unstack(x):
        out = x[0:chunk]
        for h in range(1, group_heads):
            out = out + x[h * chunk:(h + 1) * chunk]
        return out

    slots = [(bi, gi) for bi in range(batch_block) for gi in range(n_groups)]

    @pl.when(c == 0)
    def _():
        for slot, (bi, gi) in enumerate(slots):
            spread = jnp.concatenate([s0_ref[bi, gi]] * group_heads, axis=1)
            state_ref[slot] = jnp.where(state_mask, spread, 0.0)

    def lanes_of(ref, s):
        bi, gi = slots[s]
        return ref[bi, :, gi * gw:(gi + 1) * gw]

    every = range(len(slots))
    ld = [lanes_of(ld_ref, s) for s in every]
    cum = list(ld)
    shift = 1
    while shift < chunk:
        cum = [x + jnp.where(trow >= shift, pltpu.roll(x, shift, axis=0), 0.0) for x in cum]
        shift *= 2
    w_inc = [jnp.exp(x) for x in cum]
    w_exc = [jnp.exp(x - d) for x, d in zip(cum, ld)]
    w_inv = [jnp.exp(-x) for x in cum]
    w_end = [x[chunk - 1:chunk, :] for x in w_inc]

    xa = [stack(-lanes_of(kk_ref, s) * w_exc[s]).astype(BF16) for s in every]
    xr = [stack(lanes_of(r_ref, s) * w_inc[s]) for s in every]
    zb = [stack(lanes_of(kka_ref, s) * w_inv[s]).astype(BF16) for s in every]
    zk = [stack(lanes_of(k_ref, s) * w_inv[s]).astype(BF16) for s in every]
    vs = [stack(lanes_of(v_ref, s)).astype(BF16) for s in every]

    lhs = [jnp.concatenate([xa[s], xr[s].astype(BF16)], axis=0) for s in every]
    sb = [_dot_nt(lhs[s], zb[s]) for s in every]
    sk = [_dot_nt(lhs[s], zk[s]) for s in every]
    nmat = [jnp.where(strict, x[0:gc], 0.0) for x in sb]
    mrb = [jnp.where(incl, x[gc:2 * gc], 0.0).astype(BF16) for x in sb]
    lak = [jnp.where(strict, x[0:gc], 0.0).astype(BF16) for x in sk]
    mrk = [jnp.where(incl, x[gc:2 * gc], 0.0).astype(BF16) for x in sk]

    tmat = [eye + x for x in nmat]
    n_b = [x.astype(BF16) for x in nmat]
    npow = [_dot(x, x) for x in n_b]
    power = 2
    while power < chunk:
        np_b = [x.astype(BF16) for x in npow]
        if 2 * power < chunk:
            both = [_dot(jnp.concatenate([tmat[s].astype(BF16), np_b[s]], axis=0), np_b[s]) for s in every]
            tmat = [tmat[s] + both[s][0:gc] for s in every]
            npow = [x[gc:2 * gc] for x in both]
        else:
            tmat = [tmat[s] + _dot(tmat[s].astype(BF16), np_b[s]) for s in every]
        power *= 2
    tmat_b = [x.astype(BF16) for x in tmat]

    lv = [_dot(lak[s], vs[s]).astype(BF16) for s in every]
    pq = [_dot(tmat_b[s], jnp.concatenate([xa[s], lv[s]], axis=1)).astype(BF16) for s in every]
    m_pq = [_dot(mrb[s], pq[s]) for s in every]
    m_v = [_dot(mrk[s], vs[s]) for s in every]
    yp = [(xr[s] + m_pq[s][:, 0:gw]).astype(BF16) for s in every]
    yq = [m_pq[s][:, gw:2 * gw] + m_v[s] for s in every]
    gh = [_dot_tn(pq[s], zb[s]) for s in every]
    vk = [_dot_tn(vs[s], zk[s]) for s in every]
    gmat = [((eye_state + gh[s][0:gw]) * w_end[s]).astype(BF16) for s in every]
    hmat = [(gh[s][gw:2 * gw] + vk[s]) * w_end[s] for s in every]

    s_old = [state_ref[s].astype(BF16) for s in every]
    y = [unstack(_dot_nt(yp[s], s_old[s]) + yq[s]) for s in every]
    s_new = [_dot(s_old[s], gmat[s]) + hmat[s] for s in every]
    for s in every:
        bi, gi = slots[s]
        y_ref[bi, :, gi * gw:(gi + 1) * gw] = y[s]
        state_ref[s] = s_new[s]

    @pl.when(c == pl.num_programs(1) - 1)
    def _():
        for slot, (bi, gi) in enumerate(slots):
            s_new = state_ref[slot]
            folded = s_new[:, 0:R_HEAD_DIM]
            for h in range(1, group_heads):
                folded = folded + s_new[:, h * R_HEAD_DIM:(h + 1) * R_HEAD_DIM]
            sout_ref[bi, gi] = folded


def _wkv_scan(r, ld, k, v, kk, kka, s0, chunk, group_heads, batch_block):
    b, t, _ = r.shape
    gw = group_heads * R_HEAD_DIM
    n_groups = R_WIDTH // gw
    s0g = s0.reshape(b, n_groups, gw, R_HEAD_DIM)
    tok = pl.BlockSpec((batch_block, chunk, R_WIDTH), lambda i, c: (i, c, 0))
    st = pl.BlockSpec((batch_block, n_groups, gw, R_HEAD_DIM), lambda i, c: (i, 0, 0, 0))
    y, s_out = pl.pallas_call(
        functools.partial(_scan_kernel, chunk=chunk, group_heads=group_heads, batch_block=batch_block),
        out_shape=[jax.ShapeDtypeStruct((b, t, R_WIDTH), F32),
                   jax.ShapeDtypeStruct((b, n_groups, gw, R_HEAD_DIM), F32)],
        grid=(b // batch_block, t // chunk),
        in_specs=[tok] * 6 + [st],
        out_specs=[tok, st],
        scratch_shapes=[pltpu.VMEM((batch_block * n_groups, gw, gw), F32)],
        compiler_params=_params(("arbitrary", "arbitrary")),
    )(r, ld, k, v, kk, kka, s0g)
    return y, s_out.reshape(b, R_HEADS, R_HEAD_DIM, R_HEAD_DIM)


def _bias_from_distance(n, table_ref, head):
    n = jnp.maximum(n, 0)
    max_exact = NUM_BUCKETS // 2
    large = (jnp.log(jnp.maximum(n, 1).astype(F32) / max_exact)
             / math.log(MAX_DISTANCE / max_exact)
             * (NUM_BUCKETS - max_exact))
    bias = jnp.full(n.shape, table_ref[NUM_BUCKETS - 1, head], F32)
    for b in range(NUM_BUCKETS - 2, max_exact - 1, -1):
        bias = jnp.where(large < (b + 1 - max_exact), table_ref[b, head], bias)
    for b in range(max_exact):
        bias = jnp.where(n == b, table_ref[b, head], bias)
    return bias


def _lambda(lq1_ref, lk1_ref, lq2_ref, lk2_ref, lam_init):
    s1 = jnp.sum(lq1_ref[...] * lk1_ref[...], axis=-1, keepdims=True)
    s2 = jnp.sum(lq2_ref[...] * lk2_ref[...], axis=-1, keepdims=True)
    return jnp.exp(s1) - jnp.exp(s2) + lam_init


def _subln(o, g_ref, lam_init):
    o = o * lax.rsqrt(jnp.mean(o * o, axis=-1, keepdims=True) + SUBLN_EPS) * g_ref[...]
    return o * (1.0 - lam_init)


def _flash_kernel(table_ref, qt_ref, k_ref, vt_ref, lq1_ref, lk1_ref, lq2_ref, lk2_ref, gcol_ref,
                  o_ref, bias_ref, s_ref, p_ref, acc_ref, *, lam_init):
    b = pl.program_id(0)
    h = pl.program_id(1)
    qi = pl.program_id(2)
    blk = FLASH_BLOCK

    @pl.when((b == 0) & (qi == 0))
    def _():
        sub = BIAS_SUB_TILE
        far = jnp.full((sub, sub), table_ref[NUM_BUCKETS - 1, h], F32)
        di = lax.broadcasted_iota(jnp.int32, (sub, sub), 1) - lax.broadcasted_iota(jnp.int32, (sub, sub), 0)
        for tile in range(3):
            for bj in range(blk // sub):
                for bi in range(blk // sub):
                    lo = tile * blk + (bi - bj) * sub - (sub - 1)
                    if lo >= LAST_BUCKET_FROM:
                        val = far * LOG2_E
                    elif lo + 2 * (sub - 1) < 0:
                        val = jnp.full((sub, sub), NEG_INF, F32)
                    else:
                        n = di + (tile * blk + (bi - bj) * sub)
                        val = jnp.where(n >= 0, _bias_from_distance(n, table_ref, h) * LOG2_E, NEG_INF)
                    bias_ref[h, tile, bj * sub:(bj + 1) * sub, bi * sub:(bi + 1) * sub] = val

    qt = qt_ref[0]
    feat = lax.broadcasted_iota(jnp.int32, (A_V_DIM, 1), 0)
    qs = jnp.concatenate([jnp.where(feat < A_QK_DIM, qt, 0), jnp.where(feat >= A_QK_DIM, qt, 0)], axis=1)
    strip = FLASH_STRIP_ROWS
    strips = range(0, blk, strip)
    n_cols = 2 * blk

    def keys_at(j):
        return pl.ds(pl.multiple_of(j * blk, blk), blk)

    def value_product(j):
        return _dot(vt_ref[0, :, keys_at(j)], p_ref[j % 2])

    def softmax_block(j, m_old, l_old, pending_of):
        s_ref[...] = _dot(k_ref[0, keys_at(j), :], qs)
        pending = None if pending_of is None else value_product(pending_of)
        tile = jnp.minimum(qi - j, 2)

        def biased(r):
            bias = bias_ref[h, tile, r:r + strip, :]
            return s_ref[r:r + strip, :] + jnp.concatenate([bias, bias], axis=1)

        run_max = jnp.full((strip, n_cols), NEG_INF, F32)
        for r in strips:
            t = biased(r)
            s_ref[r:r + strip, :] = t
            run_max = jnp.maximum(run_max, t)
        m_new = jnp.maximum(m_old, jnp.max(run_max, axis=0, keepdims=True))
        alpha = jnp.exp2(m_old - m_new)
        run_sum = jnp.zeros((strip, n_cols), F32)
        for r in strips:
            p = jnp.exp2(s_ref[r:r + strip, :] - m_new)
            run_sum = run_sum + p
            p_ref[j % 2, r:r + strip, :] = p.astype(BF16)
        l_new = l_old * alpha + jnp.sum(run_sum, axis=0, keepdims=True)
        return m_new, l_new, alpha, pending

    m0 = jnp.full((1, n_cols), NEG_INF, F32)
    m, l, _, _ = softmax_block(0, m0, jnp.zeros((1, n_cols), F32), None)
    acc_ref[...] = jnp.zeros(acc_ref.shape, F32)

    def body(j, carry):
        m_new, l_new, alpha, pending = softmax_block(j, carry[0], carry[1], j - 1)
        acc_ref[...] = (acc_ref[...] + pending) * alpha
        return m_new, l_new

    _, l = lax.fori_loop(1, qi + 1, body, (m, l))
    lam = _lambda(lq1_ref, lk1_ref, lq2_ref, lk2_ref, lam_init)
    o = (acc_ref[...] + value_product(qi)) / l
    o = o[:, 0:blk] - lam * o[:, blk:n_cols]
    o = o * lax.rsqrt(jnp.mean(o * o, axis=0, keepdims=True) + SUBLN_EPS) * gcol_ref[...] * (1.0 - lam_init)
    o_ref[0] = o.T


def _flash_attention(qtb, kb, vtb, table, lam_vecs, subln_g, lam_init):
    b, t, _ = kb.shape
    blk = FLASH_BLOCK
    vec = pl.BlockSpec((1, A_QK_DIM), lambda bb, h, i: (0, 0))
    return pl.pallas_call(
        functools.partial(_flash_kernel, lam_init=lam_init),
        out_shape=jax.ShapeDtypeStruct((b, t, A_WIDTH), F32),
        grid=(b, A_HEADS, t // blk),
        in_specs=[
            pl.BlockSpec(memory_space=pltpu.SMEM),
            pl.BlockSpec((1, A_V_DIM, blk), lambda bb, h, i: (bb, h, i)),
            pl.BlockSpec((1, t, A_V_DIM), lambda bb, h, i: (bb, 0, h)),
            pl.BlockSpec((1, A_V_DIM, t), lambda bb, h, i: (bb, h, 0)),
            vec, vec, vec, vec,
            pl.BlockSpec((A_V_DIM, 1), lambda bb, h, i: (0, 0)),
        ],
        out_specs=pl.BlockSpec((1, blk, A_V_DIM), lambda bb, h, i: (bb, i, h)),
        scratch_shapes=[pltpu.VMEM((A_HEADS, 3, blk, blk), F32),
                        pltpu.VMEM((blk, 2 * blk), F32),
                        pltpu.VMEM((2, blk, 2 * blk), BF16),
                        pltpu.VMEM((A_V_DIM, 2 * blk), F32)],
        compiler_params=_params(("arbitrary", "arbitrary", "arbitrary")),
    )(table, qtb, kb, vtb, *lam_vecs, subln_g.reshape(A_V_DIM, 1))


def _decode_kernel(pt_ref, table_ref, q_ref, kn_ref, vn_ref, lq1_ref, lk1_ref, lq2_ref, lk2_ref, g_ref, *rest,
                   n_pages, n_new, lam_init):
    k_pages = rest[:n_pages]
    v_pages = rest[n_pages:2 * n_pages]
    o_ref = rest[2 * n_pages]
    bias_ref, bias_new_ref = rest[2 * n_pages + 1:]
    del pt_ref
    past = n_pages * PAGE_SIZE
    n_maps = 2

    @pl.when(pl.program_id(0) == 0)
    def _():
        for h in range(A_HEADS):
            tok = lax.broadcasted_iota(jnp.int32, (DECODE_ROWS, past), 0)
            pos = lax.broadcasted_iota(jnp.int32, (DECODE_ROWS, past), 1)
            bias_ref[h] = _bias_from_distance(past + tok - pos, table_ref, h)
            tok = lax.broadcasted_iota(jnp.int32, (DECODE_ROWS, PAGE_SIZE), 0)
            pos = lax.broadcasted_iota(jnp.int32, (DECODE_ROWS, PAGE_SIZE), 1)
            bias_new_ref[h] = jnp.where(pos <= tok, _bias_from_distance(tok - pos, table_ref, h), NEG_INF)

    lam = _lambda(lq1_ref, lk1_ref, lq2_ref, lk2_ref, lam_init)
    heads = range(A_HEADS)
    streams = range(A_HEADS * n_maps)
    qf = [q_ref[0, hm] for hm in streams]
    kt_all = [jnp.concatenate([kp[0, hm * A_QK_DIM:(hm + 1) * A_QK_DIM, :] for kp in k_pages],
                              axis=1).astype(BF16) for hm in streams]
    s_past = [_dot(qf[hm].astype(BF16), kt_all[hm]) + bias_ref[hm // n_maps] for hm in streams]
    s_new = [[jnp.sum(qf[hm] * kn_ref[0, hm, t:t + 1, :], axis=-1, keepdims=True)
              + bias_new_ref[hm // n_maps, :, t:t + 1] for t in range(n_new)] for hm in streams]
    mx = [jnp.max(s, axis=-1, keepdims=True) for s in s_past]
    for t in range(n_new):
        mx = [jnp.maximum(mx[hm], s_new[hm][t]) for hm in streams]
    p_past = [jnp.exp(s_past[hm] - mx[hm]) for hm in streams]
    p_new = [[jnp.exp(s - mx[hm]) for s in s_new[hm]] for hm in streams]
    norm = [jnp.sum(p, axis=-1, keepdims=True) for p in p_past]
    for t in range(n_new):
        norm = [norm[hm] + p_new[hm][t] for hm in streams]
    v_all = [jnp.concatenate([vp[0, pl.ds(h, PAGE_SIZE, stride=A_HEADS), :] for vp in v_pages],
                             axis=0).astype(BF16) for h in heads]
    acc = [_dot(jnp.concatenate([p_past[h * n_maps + m].astype(BF16) for m in range(n_maps)], axis=0), v_all[h])
           for h in heads]
    for h in heads:
        outs = []
        for m in range(n_maps):
            o = acc[h][m * DECODE_ROWS:(m + 1) * DECODE_ROWS]
            for t in range(n_new):
                o = o + p_new[h * n_maps + m][t] * vn_ref[0, h, t:t + 1, :]
            outs.append(o / norm[h * n_maps + m])
        o = _subln(outs[0] - lam * outs[1], g_ref, lam_init)
        o_ref[0, :, h * A_V_DIM:(h + 1) * A_V_DIM] = o[0:n_new]


def _decode_attention(q, k_new, v_new, cache_k, cache_v, page_table, table, lam_vecs, subln_g, lam_init):
    b, n_new, _ = q.shape
    n_pages = page_table.shape[1]
    pad = ((0, 0), (0, 0), (0, DECODE_ROWS - n_new), (0, 0))

    def per_head_map(a):
        a = a.reshape(b, n_new, 2 * A_HEADS, A_QK_DIM)
        return jnp.pad(jnp.transpose(a, (0, 2, 1, 3)), pad)

    vn = jnp.pad(jnp.transpose(v_new.reshape(b, n_new, A_HEADS, A_V_DIM), (0, 2, 1, 3)), pad)
    seq = lambda i, pt: (i, 0, 0, 0)
    fixed = lambda i, pt: (0, 0)

    def k_page(j):
        return pl.BlockSpec((1, QK_COLS, PAGE_SIZE), lambda i, pt: (pt[i, j], 0, 0))

    def v_page(j):
        return pl.BlockSpec((1, PAGE_SIZE * A_HEADS, A_V_DIM), lambda i, pt: (pt[i, j], 0, 0))

    vec = pl.BlockSpec((1, A_QK_DIM), fixed)
    grid_spec = pltpu.PrefetchScalarGridSpec(
        num_scalar_prefetch=1,
        grid=(b,),
        in_specs=[
            pl.BlockSpec(memory_space=pltpu.SMEM),
            pl.BlockSpec((1, 2 * A_HEADS, DECODE_ROWS, A_QK_DIM), seq),
            pl.BlockSpec((1, 2 * A_HEADS, DECODE_ROWS, A_QK_DIM), seq),
            pl.BlockSpec((1, A_HEADS, DECODE_ROWS, A_V_DIM), seq),
            vec, vec, vec, vec,
            pl.BlockSpec((1, A_V_DIM), fixed),
        ] + [k_page(j) for j in range(n_pages)] + [v_page(j) for j in range(n_pages)],
        out_specs=pl.BlockSpec((1, n_new, A_WIDTH), lambda i, pt: (i, 0, 0)),
        scratch_shapes=[pltpu.VMEM((A_HEADS, DECODE_ROWS, n_pages * PAGE_SIZE), F32),
                        pltpu.VMEM((A_HEADS, DECODE_ROWS, PAGE_SIZE), F32)],
    )
    return pl.pallas_call(
        functools.partial(_decode_kernel, n_pages=n_pages, n_new=n_new, lam_init=lam_init),
        out_shape=jax.ShapeDtypeStruct((b, n_new, A_WIDTH), F32),
        grid_spec=grid_spec,
        compiler_params=_params(("arbitrary",)),
    )(page_table, table, per_head_map(q), per_head_map(k_new), vn, *lam_vecs, subln_g,
      *([cache_k] * n_pages), *([cache_v] * n_pages))


def _merge_ffn_kernel(y_ref, r_ref, k_ref, v_ref, g_ref, oa_ref, h_ref, gnw_ref, gnb_ref, rk_ref, seg_ref,
                      wo_ref, gain_ref, gin_ref, gout_ref, win_ref, wo2_ref, out_ref, *, tf):
    seg = seg_ref[...]
    y = y_ref[...]
    mu = _seg_sum(y, seg) * (1.0 / R_HEAD_DIM)
    yc = y - mu
    var = _seg_sum(yc * yc, seg) * (1.0 / R_HEAD_DIM)
    yn = yc * lax.rsqrt(var + GN_EPS) * gnw_ref[...] + gnb_ref[...]
    bonus = _seg_sum(r_ref[...] * k_ref[...] * rk_ref[...], seg) * v_ref[...]
    o_rwkv = ((yn + bonus) * g_ref[...]).astype(BF16)
    m = _dot(o_rwkv, wo_ref[0:R_WIDTH, :]) + _dot(oa_ref[...].astype(BF16), wo_ref[R_WIDTH:R_WIDTH + A_WIDTH, :])
    x = h_ref[...] + _rms(m, gain_ref[...])
    out_ref[...] = _ffn_math(x, gin_ref, gout_ref, win_ref, wo2_ref, tf)


def _merge_ffn(y, r, k, v, g, o_attn, h, P, seg_ones, w_out, gain, g_in, g_out, ff_in, ff_out, tm, tf):
    n = y.shape[0]
    row = lambda i: (i, 0)
    fixed = lambda i: (0, 0)
    resident = pl.Buffered(1)
    tok = pl.BlockSpec((tm, R_WIDTH), row)
    vec = pl.BlockSpec((1, R_WIDTH), fixed)
    wide = pl.BlockSpec((1, D_MODEL), fixed)
    return pl.pallas_call(
        functools.partial(_merge_ffn_kernel, tf=tf),
        out_shape=jax.ShapeDtypeStruct((n, D_MODEL), F32),
        grid=(n // tm,),
        in_specs=[tok, tok, tok, tok, tok, tok,
                  pl.BlockSpec((tm, D_MODEL), row),
                  vec, vec, vec,
                  pl.BlockSpec((R_WIDTH, R_WIDTH), fixed, pipeline_mode=resident),
                  pl.BlockSpec((R_WIDTH + A_WIDTH, D_MODEL), fixed, pipeline_mode=resident),
                  wide, wide, wide,
                  pl.BlockSpec((D_MODEL, 2 * D_FF), fixed, pipeline_mode=resident),
                  pl.BlockSpec((D_FF, D_MODEL), fixed, pipeline_mode=resident)],
        out_specs=pl.BlockSpec((tm, D_MODEL), row),
        compiler_params=_params(("arbitrary",)),
    )(y, r, k, v, g, o_attn, h, P['gn_w'], P['gn_b'], P['rk'], seg_ones, w_out, gain, g_in, g_out, ff_in, ff_out)


def _layer(x, shift0, wkv0, attend, W, *, tm, tf, chunk, group_heads, batch_block, transposed):
    b, t, _ = x.shape
    n = b * t
    g = W['gains']
    h = _ffn_block(x.reshape(n, D_MODEL), g[0:1], g[1:2], W['ff1_in'], W['ff1_out'], tm, tf)
    if t >= tm:
        bnd = shift0.reshape(b, 1, R_COLS)
    else:
        bnd = jnp.repeat(shift0, t, axis=0)
    qb, kb, vb, k, v, tail, r, ld, kr, vr, kk, kka, gate = _proj(
        h, g[2:3], W['w_in'], W['w_q_t'], bnd, W['rw'], W['seg_ones'], tm, t, transposed)
    o_attn = attend(qb, kb, vb, k, v).reshape(n, A_WIDTH)

    def seq(a):
        return a.reshape(b, t, R_WIDTH)

    pad = (-t) % chunk
    scan_in = [seq(a) for a in (r, ld, kr, vr, kk, kka)]
    if pad:
        scan_in = [jnp.pad(a, ((0, 0), (0, pad), (0, 0))) for a in scan_in]
    y, wkv = _wkv_scan(*scan_in, wkv0, chunk, group_heads, batch_block)
    y = y[:, :t].reshape(n, R_WIDTH)
    h = _merge_ffn(y, r, kr, vr, gate, o_attn, h, W['rw'], W['seg_ones'], W['w_out'], g[3:4],
                   g[4:5], g[5:6], W['ff2_in'], W['ff2_out'], tm, tf)
    shift = tail[:, -1] if t >= tm else tail.reshape(b, t, R_COLS)[:, -1]
    if transposed:
        k = jnp.transpose(k.reshape(b, A_HEADS, 2, A_QK_DIM, t), (0, 4, 1, 2, 3))
    else:
        k = k.reshape(b, t, A_HEADS, 2, A_QK_DIM)
    return h.reshape(b, t, D_MODEL), k, v.reshape(b, t, A_HEADS, A_V_DIM), wkv, shift


def kernel(x_prompt, x_sample, cache_k, cache_v, page_table, state_wkv, state_shift, norm_gains, ff1_in, ff1_out, ff2_in, ff2_out, w_in, w_out, rel_bias_table, da_lq1, da_lk1, da_lq2, da_lk2, da_subln, rw_mu, rw_w0, rw_w2, rw_a0, rw_a2, rw_g2, rw_kk, rw_ka, rw_rk, rw_gn_w, rw_gn_b):
    depth = norm_gains.shape[0]
    b, t, _ = x_prompt.shape
    bs, ts, _ = x_sample.shape
    n_pool = cache_k.shape[1]
    lane_head = jnp.arange(R_WIDTH) // R_HEAD_DIM
    seg_ones = (lane_head[:, None] == lane_head[None, :]).astype(BF16)
    lora_pad = jnp.zeros((DECAY_LORA, R_WIDTH), BF16)

    hp, hs = x_prompt, x_sample
    outs = [[] for _ in range(8)]
    for l in range(depth):
        lam_init = 0.8 - 0.6 * math.exp(-0.3 * l)
        W = {
            'gains': norm_gains[l],
            'ff1_in': ff1_in[l].astype(BF16), 'ff1_out': ff1_out[l].astype(BF16),
            'ff2_in': ff2_in[l].astype(BF16), 'ff2_out': ff2_out[l].astype(BF16),
            'w_in': w_in[l].astype(BF16), 'w_out': w_out[l].astype(BF16),
            'w_q_t': w_in[l][:, 0:QK_COLS].T.astype(BF16),
            'seg_ones': seg_ones,
            'rw': {
                'mu': rw_mu[l][None], 'w0': rw_w0[l][None], 'a0': rw_a0[l][None],
                'w2': jnp.concatenate([rw_w2[l].astype(BF16), lora_pad], axis=0),
                'a2': jnp.concatenate([lora_pad, rw_a2[l].astype(BF16)], axis=0),
                'g2': rw_g2[l].astype(BF16),
                'kk': rw_kk[l][None], 'ka': rw_ka[l][None], 'rk': rw_rk[l].reshape(1, R_WIDTH),
                'gn_w': rw_gn_w[l][None], 'gn_b': rw_gn_b[l][None],
            },
        }
        lam_vecs = (da_lq1[l][None], da_lk1[l][None], da_lq2[l][None], da_lk2[l][None])
        subln_g = da_subln[l][None]
        pool_k = jnp.transpose(cache_k[l], (0, 2, 3, 4, 1)).reshape(n_pool, QK_COLS, PAGE_SIZE)
        pool_v = cache_v[l].reshape(n_pool, PAGE_SIZE * A_HEADS, A_V_DIM)

        def attend_p(qtb, kb, vtb, k, v):
            del k, v
            return _flash_attention(qtb, kb.reshape(b, t, QK_COLS), vtb, rel_bias_table, lam_vecs, subln_g, lam_init)

        def attend_s(qb, kb, vb, k, v):
            del kb, vb
            three = lambda a: a.reshape(bs, ts, QK_COLS)
            return _decode_attention(three(qb).astype(F32), three(k), three(v), pool_k, pool_v, page_table,
                                     rel_bias_table, lam_vecs, subln_g, lam_init)

        hp, kp, vp, wp, sp = _layer(hp, jnp.zeros((b, R_COLS), F32),
                                    jnp.zeros((b, R_HEADS, R_HEAD_DIM, R_HEAD_DIM), F32), attend_p, W,
                                    tm=512, tf=D_FF // 2, chunk=SCAN_CHUNK, group_heads=SCAN_GROUP_HEADS,
                                    batch_block=8, transposed=True)
        hs, ks, vs, ws, ss = _layer(hs, state_shift[l], state_wkv[l], attend_s, W,
                                    tm=bs * ts, tf=D_FF // 2, chunk=8, group_heads=SCAN_GROUP_HEADS, batch_block=16,
                                    transposed=False)
        for lst, val in zip(outs, (kp, vp, ks, vs, wp, ws, sp, ss)):
            lst.append(val)

    return (hp, hs) + tuple(jnp.stack(o) for o in outs)
```
